```python
import jax, jax.numpy as jnp
from jax import lax
import numpy as np

D_MODEL = 1024
BATCH = 8
SEQ = 2048
DEPTH = 1

CHUNK = 64
Q_BLOCK = 128
EPS = 1e-6

SB_HEADS = 8
SB_HEAD_DIM = 64
SB_WIDTH = SB_HEADS * SB_HEAD_DIM

DF_HEADS = 4
DF_HEAD_DIM = 64
DF_V_DIM = 2 * DF_HEAD_DIM
DF_WIDTH = DF_HEADS * DF_V_DIM
DF_QK_WIDTH = DF_HEADS * 2 * DF_HEAD_DIM

N_BRANCH = 2
IN_SIZES = (SB_WIDTH, SB_WIDTH, SB_WIDTH, SB_WIDTH,
            DF_QK_WIDTH, DF_QK_WIDTH, DF_WIDTH, DF_WIDTH)
IN_COLS = sum(IN_SIZES)
IN_SPLITS = tuple(int(v) for v in np.cumsum(IN_SIZES)[:-1])

kernel_name = "hybrid_stickbreak_diffattn_block"


def rmsnorm(x, gain):
    xf = x.astype(jnp.float32)
    y = xf * lax.rsqrt(jnp.mean(xf * xf, axis=-1, keepdims=True) + EPS)
    return (y * gain.astype(jnp.float32)).astype(x.dtype)


def alibi_slopes(n_heads):
    return np.array([2.0 ** (-8.0 * (h + 1) / n_heads) for h in range(n_heads)], dtype=np.float32)


def stick_breaking_attention(q, k, v):
    s_len, d = q.shape[1], q.shape[3]
    scale = d ** -0.5
    outs = []
    for i0 in range(0, s_len, Q_BLOCK):
        i1 = i0 + Q_BLOCK
        qb, kb, vb = q[:, i0:i1], k[:, :i1], v[:, :i1]
        z = jnp.einsum('bqhd,bkhd->bhqk', qb, kb).astype(jnp.float32) * scale
        t_idx = np.arange(i0, i1)[:, None]
        s_idx = np.arange(i1)[None, :]
        strict = s_idx < t_idx
        log_keep = jnp.where(strict, jax.nn.log_sigmoid(-z), 0.0)
        after = lax.cumsum(log_keep, axis=3, reverse=True) - log_keep
        w = jnp.where(strict, jnp.exp(jax.nn.log_sigmoid(z) + after), 0.0)
        outs.append(jnp.einsum('bhqk,bkhd->bqhd', w.astype(v.dtype), vb))
    return jnp.concatenate(outs, axis=1)


def differential_attention(q, k, v, lam, head_gain, lambda_init):
    s_len, n_heads, d = q.shape[1], q.shape[2], q.shape[4]
    scale = d ** -0.5
    slopes = alibi_slopes(n_heads)[:, None, None]
    outs = []
    for i0 in range(0, s_len, Q_BLOCK):
        i1 = i0 + Q_BLOCK
        qb, kb, vb = q[:, i0:i1], k[:, :i1], v[:, :i1]
        logits = jnp.einsum('bqhcd,bkhcd->bchqk', qb, kb).astype(jnp.float32) * scale
        t_idx = np.arange(i0, i1)[:, None]
        s_idx = np.arange(i1)[None, :]
        dist = np.abs(t_idx - s_idx).astype(np.float32)
        chunk_mask = (s_idx // CHUNK) <= (t_idx // CHUNK)
        logits = jnp.where(chunk_mask, logits - slopes * dist, -jnp.inf)
        p = jax.nn.softmax(logits, axis=-1)
        w = p[:, 0] - lam * p[:, 1]
        outs.append(jnp.einsum('bhqk,bkhe->bqhe', w.astype(v.dtype), vb))
    o = jnp.concatenate(outs, axis=1)
    return rmsnorm(o, head_gain) * (1.0 - lambda_init)


def setup_inputs(seed: int = 0) -> dict:
    key = jax.random.key(seed)
    ks = jax.random.split(key, 14)
    f32 = jnp.float32
    nrm = lambda k, shape, s: jax.random.normal(k, shape, f32) * s
    return {
        "x": jax.random.normal(ks[0], (BATCH, SEQ, D_MODEL), f32),
        "norm_pre": 1.0 + nrm(ks[1], (DEPTH, D_MODEL), 0.02),
        "norm_post": 1.0 + nrm(ks[2], (DEPTH, D_MODEL), 0.02),
        "w_in": nrm(ks[3], (DEPTH, D_MODEL, IN_COLS), D_MODEL ** -0.5),
        "w_gate": nrm(ks[4], (DEPTH, D_MODEL, N_BRANCH * D_MODEL), D_MODEL ** -0.5),
        "b_gate": nrm(ks[5], (DEPTH, N_BRANCH * D_MODEL), 0.01),
        "w_branch_sb": nrm(ks[6], (DEPTH, SB_WIDTH, D_MODEL), SB_WIDTH ** -0.5),
        "w_branch_df": nrm(ks[7], (DEPTH, DF_WIDTH, D_MODEL), DF_WIDTH ** -0.5),
        "w_out": nrm(ks[8], (DEPTH, D_MODEL, D_MODEL), D_MODEL ** -0.5),
        "lambda_q": nrm(ks[9], (DEPTH, 2, DF_HEAD_DIM), 0.1),
        "lambda_k": nrm(ks[10], (DEPTH, 2, DF_HEAD_DIM), 0.1),
        "df_head_norm": 1.0 + nrm(ks[11], (DEPTH, DF_V_DIM), 0.02),
    }


def reference(x, norm_pre, norm_post, w_in, w_gate, b_gate, w_branch_sb, w_branch_df,
              w_out, lambda_q, lambda_k, df_head_norm):
    b, s, _ = x.shape
    for l in range(DEPTH):
        lambda_init = 0.8 - 0.6 * float(np.exp(-0.3 * l))
        h = rmsnorm(x, norm_pre[l])
        proj = h @ w_in[l]
        sb_q, sb_k, sb_v, sb_z, df_q, df_k, df_v, df_z = jnp.split(proj, IN_SPLITS, axis=-1)

        sb_shape = (b, s, SB_HEADS, SB_HEAD_DIM)
        o_sb = stick_breaking_attention(sb_q.reshape(sb_shape), sb_k.reshape(sb_shape),
                                        sb_v.reshape(sb_shape)).reshape(b, s, SB_WIDTH)
        y_sb = (o_sb * jax.nn.silu(sb_z)) @ w_branch_sb[l]

        lq = lambda_q[l].astype(jnp.float32)
        lk = lambda_k[l].astype(jnp.float32)
        lam = jnp.exp(jnp.sum(lq[0] * lk[0])) - jnp.exp(jnp.sum(lq[1] * lk[1])) + lambda_init
        qk_shape = (b, s, DF_HEADS, 2, DF_HEAD_DIM)
        o_df = differential_attention(df_q.reshape(qk_shape), df_k.reshape(qk_shape),
                                      df_v.reshape(b, s, DF_HEADS, DF_V_DIM), lam,
                                      df_head_norm[l], lambda_init).reshape(b, s, DF_WIDTH)
        y_df = (o_df * jax.nn.silu(df_z)) @ w_branch_df[l]

        gates = jax.nn.sigmoid(h @ w_gate[l] + b_gate[l])
        g_sb, g_df = jnp.split(gates, N_BRANCH, axis=-1)
        out = (g_sb * y_sb + g_df * y_df) @ w_out[l]
        x = x + rmsnorm(out, norm_post[l])
    return x
```

```python
import functools

import numpy as np
import jax
import jax.numpy as jnp
from jax import lax
from jax.experimental import pallas as pl
from jax.experimental.pallas import tpu as pltpu

F32 = jnp.float32
BF16 = jnp.bfloat16

EPS = 1e-6
CHUNK = 64
HEAD_DIM = 64
LANES = 128
N_GROUPS = 4
SLAB = N_GROUPS * LANES
DF_HEADS = 4

TQ = 256
TK = 256
TM_PROJ = 512
TM_OUT = 512
NEG_BIG = -1e30

VMEM_LIMIT = 56 * 1024 * 1024


def _nt_dot(a, b):
    return lax.dot_general(a, b, (((1,), (1,)), ((), ())), preferred_element_type=F32)


def _dot(a, b):
    return jnp.dot(a, b, preferred_element_type=F32)


def _rms(x, gain):
    return x * lax.rsqrt(jnp.mean(x * x, axis=-1, keepdims=True) + EPS) * gain


def _silu(z):
    return z / (1.0 + jnp.exp(-z))


def _proj_kernel(x_ref, g_ref, w_ref, o_ref, *, col_chunk):
    h = _rms(x_ref[...], g_ref[...]).astype(BF16)
    n_cols = o_ref.shape[1]
    for n in range(0, n_cols, col_chunk):
        o_ref[:, n:n + col_chunk] = _dot(h, w_ref[:, n:n + col_chunk]).astype(BF16)


def _in_projection(xf, gain, w):
    t, d = xf.shape
    n = w.shape[1]
    return pl.pallas_call(
        functools.partial(_proj_kernel, col_chunk=512),
        grid=(t // TM_PROJ,),
        in_specs=[
            pl.BlockSpec((TM_PROJ, d), lambda i: (i, 0)),
            pl.BlockSpec((1, d), lambda i: (0, 0)),
            pl.BlockSpec((d, n), lambda i: (0, 0)),
        ],
        out_specs=pl.BlockSpec((TM_PROJ, n), lambda i: (i, 0)),
        out_shape=jax.ShapeDtypeStruct((t, n), BF16),
        compiler_params=pltpu.CompilerParams(
            dimension_semantics=("arbitrary",), vmem_limit_bytes=VMEM_LIMIT),
        name="in_projection",
    )(xf, gain, w)


def _softplus(z):
    return jnp.maximum(z, 0.0) + jnp.log(1.0 + jnp.exp(-jnp.abs(z)))


def _sb_kernel(q_ref, k_ref, v_ref, z_ref, tri_ref, o_ref, acc_ref):
    i = pl.program_id(2)
    lane = lax.broadcasted_iota(jnp.int32, (1, LANES), 1)
    first = lane < HEAD_DIM
    scale = HEAD_DIM ** -0.5
    q = q_ref[...] * scale
    zero = jnp.zeros_like(q)
    q_heads = (jnp.where(first, q, zero), jnp.where(first, zero, q))
    tri = tri_ref[...]

    row = lax.broadcasted_iota(jnp.int32, (TQ, TK), 0)
    col = lax.broadcasted_iota(jnp.int32, (TQ, TK), 1)
    strict = col < row

    def block(j, carries, diag):
        start = pl.multiple_of(j * TK, TK)
        kb = k_ref[pl.ds(start, TK), :]
        vb = v_ref[pl.ds(start, TK), :]
        vzero = jnp.zeros_like(vb)
        v_heads = (jnp.where(first, vb, vzero), jnp.where(first, vzero, vb))
        pv = None
        new_carries = []
        for hd in range(2):
            z = _nt_dot(q_heads[hd], kb)
            sp = _softplus(z)
            if diag:
                sp = jnp.where(strict, sp, 0.0)
            hi = sp.astype(BF16)
            lo = (sp - hi.astype(F32)).astype(BF16)
            suffix = _dot(hi, tri) + _dot(lo, tri)
            w = jnp.exp(z - suffix - carries[hd])
            if diag:
                w = jnp.where(strict, w, 0.0)
            new_carries.append(carries[hd] + jnp.sum(sp, axis=1, keepdims=True))
            term = _dot(w.astype(BF16), v_heads[hd])
            pv = term if pv is None else pv + term
        return pv, tuple(new_carries)

    zeros = jnp.zeros((TQ, 1), F32)
    pv, carries = block(i, (zeros, zeros), True)
    acc_ref[...] = pv

    def body(t, carries):
        pv, carries = block(i - 1 - t, carries, False)
        acc_ref[...] += pv
        return carries

    lax.fori_loop(0, i, body, carries)
    zg = z_ref[...].astype(F32)
    o_ref[...] = (acc_ref[...] * _silu(zg)).astype(BF16)


def _sb_attention(proj, tri, batch, seq):
    nq = seq // TQ
    t = proj.shape[0]
    return pl.pallas_call(
        _sb_kernel,
        grid=(batch, N_GROUPS, nq),
        in_specs=[
            pl.BlockSpec((TQ, LANES), lambda b, p, i: (b * nq + i, 0 * N_GROUPS + p)),
            pl.BlockSpec((seq, LANES), lambda b, p, i: (b, 1 * N_GROUPS + p)),
            pl.BlockSpec((seq, LANES), lambda b, p, i: (b, 2 * N_GROUPS + p)),
            pl.BlockSpec((TQ, LANES), lambda b, p, i: (b * nq + i, 3 * N_GROUPS + p)),
            pl.BlockSpec((TK, TK), lambda b, p, i: (0, 0)),
        ],
        out_specs=pl.BlockSpec((TQ, LANES), lambda b, p, i: (b * nq + i, p)),
        out_shape=jax.ShapeDtypeStruct((t, SLAB), BF16),
        scratch_shapes=[pltpu.VMEM((TQ, LANES), F32)],
        compiler_params=pltpu.CompilerParams(
            dimension_semantics=("arbitrary", "arbitrary", "arbitrary"),
            vmem_limit_bytes=VMEM_LIMIT),
        name="sb_attention",
    )(proj, proj, proj, proj, tri)


def _df_kernel(slope_ref, q_ref, k_ref, v_ref, z_ref, bias_ref, dbias_ref, lq_ref, lk_ref,
               gain_ref, o_ref, acc_ref, *, lambda_init):
    hd = pl.program_id(1)
    i = pl.program_id(2)
    lane = lax.broadcasted_iota(jnp.int32, (1, LANES), 1)
    first = lane < HEAD_DIM
    scale = HEAD_DIM ** -0.5
    q = q_ref[...] * scale
    zero = jnp.zeros_like(q)
    q_comps = (jnp.where(first, q, zero), jnp.where(first, zero, q))
    block_slope = slope_ref[hd] * TQ

    def scores(j, c, bias):
        start = pl.multiple_of(j * TK, TK)
        return _nt_dot(q_comps[c], k_ref[pl.ds(start, TK), :]) + bias

    dbias = dbias_ref[0]
    vb = v_ref[pl.ds(pl.multiple_of(i * TK, TK), TK), :]
    state = []
    for c in range(2):
        s = scores(i, c, dbias)
        m = jnp.max(s, axis=1, keepdims=True)
        p = jnp.exp(s - m)
        state += [m, jnp.sum(p, axis=1, keepdims=True)]
        acc_ref[c] = _dot(p.astype(BF16), vb)

    def body(t, state):
        j = i - 1 - t
        bias = bias_ref[0]
        shift = block_slope * (t + 1).astype(F32)
        vb = v_ref[pl.ds(pl.multiple_of(j * TK, TK), TK), :]
        new_state = []
        for c in range(2):
            m_old, l_old = state[2 * c], state[2 * c + 1]
            s = scores(j, c, bias)
            m_new = jnp.maximum(m_old, jnp.max(s, axis=1, keepdims=True) - shift)
            alpha = jnp.exp(m_old - m_new)
            p = jnp.exp(s - (m_new + shift))
            new_state += [m_new, alpha * l_old + jnp.sum(p, axis=1, keepdims=True)]
            acc_ref[c] = alpha * acc_ref[c] + _dot(p.astype(BF16), vb)
        return tuple(new_state)

    state = lax.fori_loop(0, i, body, tuple(state))

    lq = lq_ref[...]
    lk = lk_ref[...]
    dots = jnp.sum(lq * lk, axis=1, keepdims=True)
    lam = jnp.exp(dots[0:1]) - jnp.exp(dots[1:2]) + lambda_init
    o = acc_ref[0] / state[1] - lam * (acc_ref[1] / state[3])
    o = _rms(o, gain_ref[...]) * (1.0 - lambda_init)
    o_ref[...] = (o * _silu(z_ref[...].astype(F32))).astype(BF16)


def _df_attention(proj, slopes, bias, dbias, lq, lk, gain, batch, seq, lambda_init):
    nq = seq // TQ
    t = proj.shape[0]
    base = 4 * N_GROUPS
    return pl.pallas_call(
        functools.partial(_df_kernel, lambda_init=lambda_init),
        grid=(batch, DF_HEADS, nq),
        in_specs=[
            pl.BlockSpec(memory_space=pltpu.SMEM),
            pl.BlockSpec((TQ, LANES), lambda b, p, i: (b * nq + i, base + 0 * N_GROUPS + p)),
            pl.BlockSpec((seq, LANES), lambda b, p, i: (b, base + 1 * N_GROUPS + p)),
            pl.BlockSpec((seq, LANES), lambda b, p, i: (b, base + 2 * N_GROUPS + p)),
            pl.BlockSpec((TQ, LANES), lambda b, p, i: (b * nq + i, base + 3 * N_GROUPS + p)),
            pl.BlockSpec((1, TQ, TK), lambda b, p, i: (p, 0, 0)),
            pl.BlockSpec((1, TQ, TK), lambda b, p, i: (p, 0, 0)),
            pl.BlockSpec((2, HEAD_DIM), lambda b, p, i: (0, 0)),
            pl.BlockSpec((2, HEAD_DIM), lambda b, p, i: (0, 0)),
            pl.BlockSpec((1, LANES), lambda b, p, i: (0, 0)),
        ],
        out_specs=pl.BlockSpec((TQ, LANES), lambda b, p, i: (b * nq + i, p)),
        out_shape=jax.ShapeDtypeStruct((t, SLAB), BF16),
        scratch_shapes=[pltpu.VMEM((2, TQ, LANES), F32)],
        compiler_params=pltpu.CompilerParams(
            dimension_semantics=("arbitrary", "arbitrary", "arbitrary"),
            vmem_limit_bytes=VMEM_LIMIT),
        name="df_attention",
    )(slopes, proj, proj, proj, proj, bias, dbias, lq, lk, gain)


def _out_kernel(x_ref, asb_ref, adf_ref, gpre_ref, gpost_ref, wg_ref, bg_ref, wsb_ref,
                wdf_ref, wo_ref, o_ref, m_ref, *, col_chunk):
    x = x_ref[...]
    d = x.shape[1]
    h = _rms(x, gpre_ref[...]).astype(BF16)
    a_sb = asb_ref[...]
    a_df = adf_ref[...]
    for n in range(0, d, col_chunk):
        cols = slice(n, n + col_chunk)
        dcols = slice(d + n, d + n + col_chunk)
        g_sb = jax.nn.sigmoid(_dot(h, wg_ref[:, cols]) + bg_ref[:, cols])
        g_df = jax.nn.sigmoid(_dot(h, wg_ref[:, dcols]) + bg_ref[:, dcols])
        y_sb = _dot(a_sb, wsb_ref[:, cols])
        y_df = _dot(a_df, wdf_ref[:, cols])
        m_ref[:, cols] = (g_sb * y_sb + g_df * y_df).astype(BF16)
    out = _dot(m_ref[...], wo_ref[...])
    o_ref[...] = x + _rms(out, gpost_ref[...])


def _output_stage(xf, a_sb, a_df, gpre, gpost, wg, bg, wsb, wdf, wo):
    t, d = xf.shape
    const = lambda i: (0, 0)
    return pl.pallas_call(
        functools.partial(_out_kernel, col_chunk=256),
        grid=(t // TM_OUT,),
        in_specs=[
            pl.BlockSpec((TM_OUT, d), lambda i: (i, 0)),
            pl.BlockSpec((TM_OUT, SLAB), lambda i: (i, 0)),
            pl.BlockSpec((TM_OUT, SLAB), lambda i: (i, 0)),
            pl.BlockSpec((1, d), const),
            pl.BlockSpec((1, d), const),
            pl.BlockSpec(wg.shape, const),
            pl.BlockSpec(bg.shape, const),
            pl.BlockSpec(wsb.shape, const),
            pl.BlockSpec(wdf.shape, const),
            pl.BlockSpec(wo.shape, const),
        ],
        out_specs=pl.BlockSpec((TM_OUT, d), lambda i: (i, 0)),
        out_shape=jax.ShapeDtypeStruct((t, d), F32),
        scratch_shapes=[pltpu.VMEM((TM_OUT, d), BF16)],
        compiler_params=pltpu.CompilerParams(
            dimension_semantics=("arbitrary",), vmem_limit_bytes=VMEM_LIMIT),
        name="output_stage",
    )(xf, a_sb, a_df, gpre, gpost, wg, bg, wsb, wdf, wo)


def _constant_tiles():
    r = np.arange(TQ)[:, None]
    c = np.arange(TK)[None, :]
    tri = (r >= c).astype(np.float32)
    slopes = np.array([2.0 ** (-8.0 * (h + 1) / DF_HEADS) for h in range(DF_HEADS)],
                      dtype=np.float32)
    rel = (r - c).astype(np.float32)
    bias = -slopes[:, None, None] * rel[None]
    visible = (c // CHUNK) <= (r // CHUNK)
    dbias = np.where(visible[None], -slopes[:, None, None] * np.abs(rel)[None], NEG_BIG)
    return (jnp.asarray(tri, BF16), jnp.asarray(slopes), jnp.asarray(bias, F32),
            jnp.asarray(dbias, F32))


def kernel(x, norm_pre, norm_post, w_in, w_gate, b_gate, w_branch_sb, w_branch_df, w_out,
           lambda_q, lambda_k, df_head_norm):
    b, s, d = x.shape
    depth = w_in.shape[0]
    assert s % TQ == 0 and TQ == TK and TQ % CHUNK == 0
    assert (b * s) % TM_PROJ == 0 and (b * s) % TM_OUT == 0
    tri, slopes, bias, dbias = _constant_tiles()
    xf = x.reshape(b * s, d)
    for l in range(depth):
        lambda_init = 0.8 - 0.6 * float(np.exp(-0.3 * l))
        proj = _in_projection(xf, norm_pre[l][None], w_in[l].astype(BF16))
        a_sb = _sb_attention(proj, tri, b, s)
        a_df = _df_attention(proj, slopes, bias, dbias, lambda_q[l].astype(F32),
                             lambda_k[l].astype(F32), df_head_norm[l][None].astype(F32),
                             b, s, lambda_init)
        xf = _output_stage(xf, a_sb, a_df, norm_pre[l][None], norm_post[l][None],
                           w_gate[l].astype(BF16), b_gate[l][None],
                           w_branch_sb[l].astype(BF16), w_branch_df[l].astype(BF16),
                           w_out[l].astype(BF16))
    return xf.reshape(b, s, d)
```

```python
import functools
import math

import numpy as np
import jax
import jax.numpy as jnp
from jax import lax
from jax.experimental import pallas as pl
from jax.experimental.pallas import tpu as pltpu

F32 = jnp.float32
BF16 = jnp.bfloat16

EPS = 1e-6
CHUNK = 64
HEAD_DIM = 64
LANES = 128
N_GROUPS = 4
SLAB = N_GROUPS * LANES
SB_HEADS = SLAB // HEAD_DIM
DF_HEADS = 4
LOG2E = math.log2(math.e)
Q_SCALE = HEAD_DIM ** -0.5 * LOG2E

TQ = 256
TK = 256
TM_PROJ = 512
TM_OUT = 512
NEG_BIG = -1e30
SKIP_LOG2 = 140.0
BOUND_MARGIN = 1.01

VMEM_LIMIT = 56 * 1024 * 1024


def _nt_dot(a, b):
    return lax.dot_general(a, b, (((1,), (1,)), ((), ())), preferred_element_type=F32)


def _dot(a, b):
    return jnp.dot(a, b, preferred_element_type=F32)


def _rms(x, gain):
    return x * lax.rsqrt(jnp.mean(x * x, axis=-1, keepdims=True) + EPS) * gain


def _silu(z):
    return z / (1.0 + jnp.exp(-z))


def _lane_halves(x):
    lane = lax.broadcasted_iota(jnp.int32, (1, LANES), 1)
    first = lane < HEAD_DIM
    zero = jnp.zeros_like(x)
    return jnp.where(first, x, zero), jnp.where(first, zero, x)


def _proj_kernel(x_ref, g_ref, w_ref, o_ref, *, col_chunk, q_chunks):
    h = _rms(x_ref[...], g_ref[...]).astype(BF16)
    n_cols = o_ref.shape[1]
    for n in range(0, n_cols, col_chunk):
        y = _dot(h, w_ref[:, n:n + col_chunk])
        if n // col_chunk in q_chunks:
            y = y * Q_SCALE
        o_ref[:, n:n + col_chunk] = y.astype(BF16)


def _in_projection(xf, gain, w):
    t, d = xf.shape
    n = w.shape[1]
    return pl.pallas_call(
        functools.partial(_proj_kernel, col_chunk=SLAB, q_chunks=(0, 4)),
        grid=(t // TM_PROJ,),
        in_specs=[
            pl.BlockSpec((TM_PROJ, d), lambda i: (i, 0)),
            pl.BlockSpec((1, d), lambda i: (0, 0)),
            pl.BlockSpec((d, n), lambda i: (0, 0)),
        ],
        out_specs=pl.BlockSpec((TM_PROJ, n), lambda i: (i, 0)),
        out_shape=jax.ShapeDtypeStruct((t, n), BF16),
        compiler_params=pltpu.CompilerParams(
            dimension_semantics=("arbitrary",), vmem_limit_bytes=VMEM_LIMIT),
        name="in_projection",
    )(xf, gain, w)


def _softplus2(z2):
    return jnp.maximum(z2, 0.0) + jnp.log2(1.0 + jnp.exp2(-jnp.abs(z2)))


def _sb_kernel(q_ref, k_ref, v_ref, z_ref, tri_ref, gsum_ref, o_ref, acc_ref, kn_ref):
    i = pl.program_id(1)
    seq = k_ref.shape[0]

    @pl.when(i == 0)
    def _():
        kn = jnp.zeros((1, LANES), F32)
        for r in range(0, seq, TK):
            kf = k_ref[r:r + TK, :].astype(F32)
            n2 = _dot((kf * kf).astype(BF16), gsum_ref[...])
            kn = jnp.maximum(kn, jnp.max(n2, axis=0, keepdims=True))
        kn_ref[...] = kn

    qf = q_ref[...].astype(F32)
    qn = _dot((qf * qf).astype(BF16), gsum_ref[...])
    zbound = jnp.sqrt(qn * kn_ref[...]) * BOUND_MARGIN
    zb = [zbound[:, h:h + 1] for h in range(SB_HEADS)]

    q_heads = []
    for g in range(N_GROUPS):
        q_heads += list(_lane_halves(q_ref[:, g * LANES:(g + 1) * LANES]))
    tri = tri_ref[...]

    row = lax.broadcasted_iota(jnp.int32, (TQ, TK), 0)
    col = lax.broadcasted_iota(jnp.int32, (TQ, TK), 1)
    strict = col < row

    def block(j, carries, diag):
        start = pl.multiple_of(j * TK, TK)
        new_carries = []
        slack = None
        for g in range(N_GROUPS):
            cols = slice(g * LANES, (g + 1) * LANES)
            kb = k_ref[pl.ds(start, TK), cols]
            v_heads = _lane_halves(v_ref[pl.ds(start, TK), cols])
            pv = None
            for sub in range(2):
                h = 2 * g + sub
                z = _nt_dot(q_heads[h], kb)
                sp = _softplus2(z)
                if diag:
                    sp = jnp.where(strict, sp, 0.0)
                hi = sp.astype(BF16)
                lo = (sp - hi.astype(F32)).astype(BF16)
                suffix = _dot(hi, tri) + _dot(lo, tri)
                w = jnp.exp2(z - suffix - carries[h])
                if diag:
                    w = jnp.where(strict, w, 0.0)
                carry = carries[h] + jnp.sum(sp, axis=1, keepdims=True)
                new_carries.append(carry)
                room = carry - zb[h]
                slack = room if slack is None else jnp.minimum(slack, room)
                term = _dot(w.astype(BF16), v_heads[sub])
                pv = term if pv is None else pv + term
            if diag:
                acc_ref[g] = pv
            else:
                acc_ref[g] += pv
        return tuple(new_carries), jnp.min(slack)

    zeros = jnp.zeros((TQ, 1), F32)
    carries, slack = block(i, (zeros,) * SB_HEADS, True)

    def cond(state):
        return jnp.logical_and(state[0] < i, state[1] < SKIP_LOG2)

    def body(state):
        t = state[0]
        carries, slack = block(i - 1 - t, state[2:], False)
        return (t + 1, slack) + carries

    lax.while_loop(cond, body, (jnp.int32(0), slack) + carries)

    for g in range(N_GROUPS):
        cols = slice(g * LANES, (g + 1) * LANES)
        o_ref[:, cols] = (acc_ref[g] * _silu(z_ref[:, cols].astype(F32))).astype(BF16)


def _sb_attention(proj, tri, gsum, batch, seq):
    nq = seq // TQ
    t = proj.shape[0]
    return pl.pallas_call(
        _sb_kernel,
        grid=(batch, nq),
        in_specs=[
            pl.BlockSpec((TQ, SLAB), lambda b, i: (b * nq + i, 0)),
            pl.BlockSpec((seq, SLAB), lambda b, i: (b, 1)),
            pl.BlockSpec((seq, SLAB), lambda b, i: (b, 2)),
            pl.BlockSpec((TQ, SLAB), lambda b, i: (b * nq + i, 3)),
            pl.BlockSpec((TK, TK), lambda b, i: (0, 0)),
            pl.BlockSpec((SLAB, LANES), lambda b, i: (0, 0)),
        ],
        out_specs=pl.BlockSpec((TQ, SLAB), lambda b, i: (b * nq + i, 0)),
        out_shape=jax.ShapeDtypeStruct((t, SLAB), BF16),
        scratch_shapes=[pltpu.VMEM((N_GROUPS, TQ, LANES), F32),
                        pltpu.VMEM((1, LANES), F32)],
        compiler_params=pltpu.CompilerParams(
            dimension_semantics=("arbitrary", "arbitrary"), vmem_limit_bytes=VMEM_LIMIT),
        name="sb_attention",
    )(proj, proj, proj, proj, tri, gsum)


def _df_kernel(slope_ref, q_ref, k_ref, v_ref, z_ref, bias_ref, dbias_ref, lq_ref, lk_ref,
               gain_ref, o_ref, acc_ref, *, lambda_init):
    i = pl.program_id(1)
    q_comps = []
    for hd in range(DF_HEADS):
        q_comps += list(_lane_halves(q_ref[:, hd * LANES:(hd + 1) * LANES]))
    n_chain = 2 * DF_HEADS

    def scores(j, c, bias):
        hd = c // 2
        start = pl.multiple_of(j * TK, TK)
        return _nt_dot(q_comps[c], k_ref[pl.ds(start, TK), hd * LANES:(hd + 1) * LANES]) + bias

    def values(j, hd):
        return v_ref[pl.ds(pl.multiple_of(j * TK, TK), TK), hd * LANES:(hd + 1) * LANES]

    state = []
    for c in range(n_chain):
        s = scores(i, c, dbias_ref[c // 2])
        m = jnp.max(s, axis=1, keepdims=True)
        p = jnp.exp2(s - m)
        state += [m, jnp.sum(p, axis=1, keepdims=True)]
        acc_ref[c] = _dot(p.astype(BF16), values(i, c // 2))

    def body(t, state):
        j = i - 1 - t
        dist = (t + 1).astype(F32)
        new_state = []
        for c in range(n_chain):
            hd = c // 2
            shift = slope_ref[hd] * dist
            m_old, l_old = state[2 * c], state[2 * c + 1]
            s = scores(j, c, bias_ref[hd])
            m_new = jnp.maximum(m_old, jnp.max(s, axis=1, keepdims=True) - shift)
            alpha = jnp.exp2(m_old - m_new)
            p = jnp.exp2(s - (m_new + shift))
            new_state += [m_new, alpha * l_old + jnp.sum(p, axis=1, keepdims=True)]
            acc_ref[c] = alpha * acc_ref[c] + _dot(p.astype(BF16), values(j, hd))
        return tuple(new_state)

    state = lax.fori_loop(0, i, body, tuple(state))

    dots = jnp.sum(lq_ref[...] * lk_ref[...], axis=1, keepdims=True)
    lam = jnp.exp(dots[0:1]) - jnp.exp(dots[1:2]) + lambda_init
    for hd in range(DF_HEADS):
        cols = slice(hd * LANES, (hd + 1) * LANES)
        o = acc_ref[2 * hd] / state[4 * hd + 1] - lam * (acc_ref[2 * hd + 1] / state[4 * hd + 3])
        o = _rms(o, gain_ref[...]) * (1.0 - lambda_init)
        o_ref[:, cols] = (o * _silu(z_ref[:, cols].astype(F32))).astype(BF16)


def _df_attention(proj, slopes, bias, dbias, lq, lk, gain, batch, seq, lambda_init):
    nq = seq // TQ
    t = proj.shape[0]
    const3 = lambda b, i: (0, 0, 0)
    const2 = lambda b, i: (0, 0)
    return pl.pallas_call(
        functools.partial(_df_kernel, lambda_init=lambda_init),
        grid=(batch, nq),
        in_specs=[
            pl.BlockSpec(memory_space=pltpu.SMEM),
            pl.BlockSpec((TQ, SLAB), lambda b, i: (b * nq + i, 4)),
            pl.BlockSpec((seq, SLAB), lambda b, i: (b, 5)),
            pl.BlockSpec((seq, SLAB), lambda b, i: (b, 6)),
            pl.BlockSpec((TQ, SLAB), lambda b, i: (b * nq + i, 7)),
            pl.BlockSpec((DF_HEADS, TQ, TK), const3),
            pl.BlockSpec((DF_HEADS, TQ, TK), const3),
            pl.BlockSpec((2, HEAD_DIM), const2),
            pl.BlockSpec((2, HEAD_DIM), const2),
            pl.BlockSpec((1, LANES), const2),
        ],
        out_specs=pl.BlockSpec((TQ, SLAB), lambda b, i: (b * nq + i, 0)),
        out_shape=jax.ShapeDtypeStruct((t, SLAB), BF16),
        scratch_shapes=[pltpu.VMEM((2 * DF_HEADS, TQ, LANES), F32)],
        compiler_params=pltpu.CompilerParams(
            dimension_semantics=("arbitrary", "arbitrary"), vmem_limit_bytes=VMEM_LIMIT),
        name="df_attention",
    )(slopes, proj, proj, proj, proj, bias, dbias, lq, lk, gain)


def _out_kernel(x_ref, asb_ref, adf_ref, gpre_ref, gpost_ref, wg_ref, bg_ref, wsb_ref,
                wdf_ref, wo_ref, o_ref, m_ref, *, col_chunk):
    x = x_ref[...]
    d = x.shape[1]
    h = _rms(x, gpre_ref[...]).astype(BF16)
    a_sb = asb_ref[...]
    a_df = adf_ref[...]
    for n in range(0, d, col_chunk):
        cols = slice(n, n + col_chunk)
        dcols = slice(d + n, d + n + col_chunk)
        g_sb = jax.nn.sigmoid(_dot(h, wg_ref[:, cols]) + bg_ref[:, cols])
        g_df = jax.nn.sigmoid(_dot(h, wg_ref[:, dcols]) + bg_ref[:, dcols])
        y_sb = _dot(a_sb, wsb_ref[:, cols])
        y_df = _dot(a_df, wdf_ref[:, cols])
        m_ref[:, cols] = (g_sb * y_sb + g_df * y_df).astype(BF16)
    out = _dot(m_ref[...], wo_ref[...])
    o_ref[...] = x + _rms(out, gpost_ref[...])


def _output_stage(xf, a_sb, a_df, gpre, gpost, wg, bg, wsb, wdf, wo):
    t, d = xf.shape
    const = lambda i: (0, 0)
    return pl.pallas_call(
        functools.partial(_out_kernel, col_chunk=256),
        grid=(t // TM_OUT,),
        in_specs=[
            pl.BlockSpec((TM_OUT, d), lambda i: (i, 0)),
            pl.BlockSpec((TM_OUT, SLAB), lambda i: (i, 0)),
            pl.BlockSpec((TM_OUT, SLAB), lambda i: (i, 0)),
            pl.BlockSpec((1, d), const),
            pl.BlockSpec((1, d), const),
            pl.BlockSpec(wg.shape, const),
            pl.BlockSpec(bg.shape, const),
            pl.BlockSpec(wsb.shape, const),
            pl.BlockSpec(wdf.shape, const),
            pl.BlockSpec(wo.shape, const),
        ],
        out_specs=pl.BlockSpec((TM_OUT, d), lambda i: (i, 0)),
        out_shape=jax.ShapeDtypeStruct((t, d), F32),
        scratch_shapes=[pltpu.VMEM((TM_OUT, d), BF16)],
        compiler_params=pltpu.CompilerParams(
            dimension_semantics=("arbitrary",), vmem_limit_bytes=VMEM_LIMIT),
        name="output_stage",
    )(xf, a_sb, a_df, gpre, gpost, wg, bg, wsb, wdf, wo)


def _constant_tiles():
    r = np.arange(TQ)[:, None]
    c = np.arange(TK)[None, :]
    tri = (r >= c).astype(np.float32)
    gsum = (np.arange(SLAB)[:, None] // HEAD_DIM == np.arange(LANES)[None, :]).astype(np.float32)
    slopes = np.array([2.0 ** (-8.0 * (h + 1) / DF_HEADS) for h in range(DF_HEADS)],
                      dtype=np.float64) * LOG2E
    rel = (r - c).astype(np.float64)
    bias = -slopes[:, None, None] * rel[None]
    visible = (c // CHUNK) <= (r // CHUNK)
    dbias = np.where(visible[None], -slopes[:, None, None] * np.abs(rel)[None], NEG_BIG)
    block_slopes = slopes * TQ
    return (jnp.asarray(tri, BF16), jnp.asarray(gsum, BF16), jnp.asarray(block_slopes, F32),
            jnp.asarray(bias, F32), jnp.asarray(dbias, F32))


def kernel(x, norm_pre, norm_post, w_in, w_gate, b_gate, w_branch_sb, w_branch_df, w_out,
           lambda_q, lambda_k, df_head_norm):
    b, s, d = x.shape
    depth = w_in.shape[0]
    assert s % TQ == 0 and TQ == TK and TQ % CHUNK == 0
    assert (b * s) % TM_PROJ == 0 and (b * s) % TM_OUT == 0
    tri, gsum, block_slopes, bias, dbias = _constant_tiles()
    xf = x.reshape(b * s, d)
    for l in range(depth):
        lambda_init = 0.8 - 0.6 * float(np.exp(-0.3 * l))
        proj = _in_projection(xf, norm_pre[l][None], w_in[l].astype(BF16))
        a_sb = _sb_attention(proj, tri, gsum, b, s)
        a_df = _df_attention(proj, block_slopes, bias, dbias, lambda_q[l].astype(F32),
                             lambda_k[l].astype(F32), df_head_norm[l][None].astype(F32),
                             b, s, lambda_init)
        xf = _output_stage(xf, a_sb, a_df, norm_pre[l][None], norm_post[l][None],
                           w_gate[l].astype(BF16), b_gate[l][None],
                           w_branch_sb[l].astype(BF16), w_branch_df[l].astype(BF16),
                           w_out[l].astype(BF16))
    return xf.reshape(b, s, d)
```

```python
import functools
import math

import numpy as np
import jax
import jax.numpy as jnp
from jax import lax
from jax.experimental import pallas as pl
from jax.experimental.pallas import tpu as pltpu

F32 = jnp.float32
BF16 = jnp.bfloat16

EPS = 1e-6
CHUNK = 64
HEAD_DIM = 64
LANES = 128
N_GROUPS = 4
SLAB = N_GROUPS * LANES
SB_HEADS = SLAB // HEAD_DIM
DF_HEADS = 4
LOG2E = math.log2(math.e)
Q_SCALE = HEAD_DIM ** -0.5 * LOG2E

TQ = 256
TK = 256
TM_PROJ = 512
TM_OUT = 512
NEG_BIG = -1e30
SKIP_LOG2 = 140.0
BOUND_MARGIN = 1.01
MAX_FIXED_REF_LOG2 = 100.0

VMEM_LIMIT = 56 * 1024 * 1024


def _nt_dot(a, b):
    return lax.dot_general(a, b, (((1,), (1,)), ((), ())), preferred_element_type=F32)


def _dot(a, b):
    return jnp.dot(a, b, preferred_element_type=F32)


def _rms(x, gain):
    return x * lax.rsqrt(jnp.mean(x * x, axis=-1, keepdims=True) + EPS) * gain


def _silu(z):
    return z / (1.0 + jnp.exp(-z))


def _lane_halves(x):
    lane = lax.broadcasted_iota(jnp.int32, (1, LANES), 1)
    first = lane < HEAD_DIM
    zero = jnp.zeros_like(x)
    return jnp.where(first, x, zero), jnp.where(first, zero, x)


def _max_sq_key_norms(k_ref, gsum_ref):
    kn = jnp.zeros((1, LANES), F32)
    for r in range(0, k_ref.shape[0], TK):
        kf = k_ref[r:r + TK, :].astype(F32)
        n2 = _dot((kf * kf).astype(BF16), gsum_ref[...])
        kn = jnp.maximum(kn, jnp.max(n2, axis=0, keepdims=True))
    return kn


def _logit_bounds(q_ref, kn_ref, gsum_ref):
    qf = q_ref[...].astype(F32)
    qn = jnp.max(_dot((qf * qf).astype(BF16), gsum_ref[...]), axis=0, keepdims=True)
    return jnp.sqrt(qn * kn_ref[...]) * BOUND_MARGIN


def _lane_pack(column_values):
    lane = lax.broadcasted_iota(jnp.int32, (1, LANES), 1)
    row = jnp.zeros((1, LANES), F32)
    for h, v in enumerate(column_values):
        row = jnp.where(lane == h, v, row)
    return row


def _lane_reduce(row, n, op, fill):
    lane = lax.broadcasted_iota(jnp.int32, (1, LANES), 1)
    return op(jnp.where(lane < n, row, fill))


def _proj_kernel(x_ref, g_ref, w_ref, o_ref, *, col_chunk, q_chunks):
    h = _rms(x_ref[...], g_ref[...]).astype(BF16)
    n_cols = o_ref.shape[1]
    for n in range(0, n_cols, col_chunk):
        y = _dot(h, w_ref[:, n:n + col_chunk])
        if n // col_chunk in q_chunks:
            y = y * Q_SCALE
        o_ref[:, n:n + col_chunk] = y.astype(BF16)


def _in_projection(xf, gain, w):
    t, d = xf.shape
    n = w.shape[1]
    return pl.pallas_call(
        functools.partial(_proj_kernel, col_chunk=SLAB, q_chunks=(0, 4)),
        grid=(t // TM_PROJ,),
        in_specs=[
            pl.BlockSpec((TM_PROJ, d), lambda i: (i, 0)),
            pl.BlockSpec((1, d), lambda i: (0, 0)),
            pl.BlockSpec((d, n), lambda i: (0, 0)),
        ],
        out_specs=pl.BlockSpec((TM_PROJ, n), lambda i: (i, 0)),
        out_shape=jax.ShapeDtypeStruct((t, n), BF16),
        compiler_params=pltpu.CompilerParams(
            dimension_semantics=("arbitrary",), vmem_limit_bytes=VMEM_LIMIT),
        name="in_projection",
    )(xf, gain, w)


def _softplus2(z2):
    return jnp.maximum(z2, 0.0) + jnp.log2(1.0 + jnp.exp2(-jnp.abs(z2)))


def _sb_kernel(q_ref, k_ref, v_ref, z_ref, tri_ref, gsum_ref, o_ref, acc_ref, kn_ref):
    i = pl.program_id(1)

    @pl.when(i == 0)
    def _():
        kn_ref[...] = _max_sq_key_norms(k_ref, gsum_ref)

    zbound = _logit_bounds(q_ref, kn_ref, gsum_ref)

    q_heads = []
    for g in range(N_GROUPS):
        q_heads += list(_lane_halves(q_ref[:, g * LANES:(g + 1) * LANES]))
    tri = tri_ref[...]

    row = lax.broadcasted_iota(jnp.int32, (TQ, TK), 0)
    col = lax.broadcasted_iota(jnp.int32, (TQ, TK), 1)
    strict = col < row

    def block(j, carries, diag):
        start = pl.multiple_of(j * TK, TK)
        new_carries = []
        for g in range(N_GROUPS):
            cols = slice(g * LANES, (g + 1) * LANES)
            kb = k_ref[pl.ds(start, TK), cols]
            v_heads = _lane_halves(v_ref[pl.ds(start, TK), cols])
            pv = None
            for sub in range(2):
                h = 2 * g + sub
                z = _nt_dot(q_heads[h], kb)
                sp = _softplus2(z)
                if diag:
                    sp = jnp.where(strict, sp, 0.0)
                hi = sp.astype(BF16)
                lo = (sp - hi.astype(F32)).astype(BF16)
                suffix = _dot(hi, tri) + _dot(lo, tri)
                w = jnp.exp2(z - suffix - carries[h])
                if diag:
                    w = jnp.where(strict, w, 0.0)
                carry = carries[h] + jnp.sum(sp, axis=1, keepdims=True)
                new_carries.append(carry)
                term = _dot(w.astype(BF16), v_heads[sub])
                pv = term if pv is None else pv + term
            if diag:
                acc_ref[g] = pv
            else:
                acc_ref[g] += pv
        least = _lane_pack([jnp.min(c, axis=0, keepdims=True) for c in new_carries])
        slack = _lane_reduce(least - zbound, SB_HEADS, jnp.min, jnp.inf)
        return tuple(new_carries), slack

    zeros = jnp.zeros((TQ, 1), F32)
    carries, slack = block(i, (zeros,) * SB_HEADS, True)

    def cond(state):
        return jnp.logical_and(state[0] < i, state[1] < SKIP_LOG2)

    def body(state):
        t = state[0]
        carries, slack = block(i - 1 - t, state[2:], False)
        return (t + 1, slack) + carries

    lax.while_loop(cond, body, (jnp.int32(0), slack) + carries)

    for g in range(N_GROUPS):
        cols = slice(g * LANES, (g + 1) * LANES)
        o_ref[:, cols] = (acc_ref[g] * _silu(z_ref[:, cols].astype(F32))).astype(BF16)


def _sb_attention(proj, tri, gsum, batch, seq):
    nq = seq // TQ
    t = proj.shape[0]
    return pl.pallas_call(
        _sb_kernel,
        grid=(batch, nq),
        in_specs=[
            pl.BlockSpec((TQ, SLAB), lambda b, i: (b * nq + i, 0)),
            pl.BlockSpec((seq, SLAB), lambda b, i: (b, 1)),
            pl.BlockSpec((seq, SLAB), lambda b, i: (b, 2)),
            pl.BlockSpec((TQ, SLAB), lambda b, i: (b * nq + i, 3)),
            pl.BlockSpec((TK, TK), lambda b, i: (0, 0)),
            pl.BlockSpec((SLAB, LANES), lambda b, i: (0, 0)),
        ],
        out_specs=pl.BlockSpec((TQ, SLAB), lambda b, i: (b * nq + i, 0)),
        out_shape=jax.ShapeDtypeStruct((t, SLAB), BF16),
        scratch_shapes=[pltpu.VMEM((N_GROUPS, TQ, LANES), F32),
                        pltpu.VMEM((1, LANES), F32)],
        compiler_params=pltpu.CompilerParams(
            dimension_semantics=("arbitrary", "arbitrary"), vmem_limit_bytes=VMEM_LIMIT),
        name="sb_attention",
    )(proj, proj, proj, proj, tri, gsum)


def _df_kernel(slope_ref, q_ref, k_ref, v_ref, z_ref, bias_ref, dbias_ref, lq_ref, lk_ref,
               gain_ref, gsum_ref, o_ref, acc_ref, kn_ref, *, lambda_init):
    i = pl.program_id(1)
    n_chain = 2 * DF_HEADS

    @pl.when(i == 0)
    def _():
        kn_ref[...] = _max_sq_key_norms(k_ref, gsum_ref)

    lbound = _logit_bounds(q_ref, kn_ref, gsum_ref)
    q_comps = []
    for hd in range(DF_HEADS):
        q_comps += list(_lane_halves(q_ref[:, hd * LANES:(hd + 1) * LANES]))

    def scores(j, c, bias):
        hd = c // 2
        start = pl.multiple_of(j * TK, TK)
        return _nt_dot(q_comps[c], k_ref[pl.ds(start, TK), hd * LANES:(hd + 1) * LANES]) + bias

    ones = jnp.ones((TK, LANES), BF16)

    def values(j, hd):
        vb = v_ref[pl.ds(pl.multiple_of(j * TK, TK), TK), hd * LANES:(hd + 1) * LANES]
        return jnp.concatenate([vb, ones], axis=1)

    ms = []
    for c in range(n_chain):
        s = scores(i, c, dbias_ref[c // 2])
        m = jnp.max(s, axis=1, keepdims=True)
        ms.append(m)
        acc_ref[c] = _dot(jnp.exp2(s - m).astype(BF16), values(i, c // 2))
    least = _lane_pack([jnp.min(m, axis=0, keepdims=True) for m in ms])
    gap = _lane_reduce(lbound - least, n_chain, jnp.max, -jnp.inf)
    fixed_ref_ok = gap <= MAX_FIXED_REF_LOG2

    def block_shift(t, hd):
        return slope_ref[hd] * (t + 1).astype(F32)

    @pl.when(fixed_ref_ok)
    def _():
        def body(t, carry):
            j = i - 1 - t
            for c in range(n_chain):
                hd = c // 2
                p = jnp.exp2(scores(j, c, bias_ref[hd]) - (ms[c] + block_shift(t, hd)))
                acc_ref[c] += _dot(p.astype(BF16), values(j, hd))
            return carry

        lax.fori_loop(0, i, body, 0)

    @pl.when(jnp.logical_not(fixed_ref_ok))
    def _():
        def body(t, m_run):
            j = i - 1 - t
            m_out = []
            for c in range(n_chain):
                hd = c // 2
                shift = block_shift(t, hd)
                s = scores(j, c, bias_ref[hd])
                m_new = jnp.maximum(m_run[c], jnp.max(s, axis=1, keepdims=True) - shift)
                alpha = jnp.exp2(m_run[c] - m_new)
                p = jnp.exp2(s - (m_new + shift))
                m_out.append(m_new)
                acc_ref[c] = alpha * acc_ref[c] + _dot(p.astype(BF16), values(j, hd))
            return tuple(m_out)

        lax.fori_loop(0, i, body, tuple(ms))

    dots = jnp.sum(lq_ref[...] * lk_ref[...], axis=1, keepdims=True)
    lam = jnp.exp(dots[0:1]) - jnp.exp(dots[1:2]) + lambda_init
    for hd in range(DF_HEADS):
        cols = slice(hd * LANES, (hd + 1) * LANES)
        a0, a1 = acc_ref[2 * hd], acc_ref[2 * hd + 1]
        o = a0[:, :LANES] / a0[:, LANES:] - lam * (a1[:, :LANES] / a1[:, LANES:])
        o = _rms(o, gain_ref[...]) * (1.0 - lambda_init)
        o_ref[:, cols] = (o * _silu(z_ref[:, cols].astype(F32))).astype(BF16)


def _df_attention(proj, slopes, bias, dbias, lq, lk, gain, gsum, batch, seq, lambda_init):
    nq = seq // TQ
    t = proj.shape[0]
    const3 = lambda b, i: (0, 0, 0)
    const2 = lambda b, i: (0, 0)
    return pl.pallas_call(
        functools.partial(_df_kernel, lambda_init=lambda_init),
        grid=(batch, nq),
        in_specs=[
            pl.BlockSpec(memory_space=pltpu.SMEM),
            pl.BlockSpec((TQ, SLAB), lambda b, i: (b * nq + i, 4)),
            pl.BlockSpec((seq, SLAB), lambda b, i: (b, 5)),
            pl.BlockSpec((seq, SLAB), lambda b, i: (b, 6)),
            pl.BlockSpec((TQ, SLAB), lambda b, i: (b * nq + i, 7)),
            pl.BlockSpec((DF_HEADS, TQ, TK), const3),
            pl.BlockSpec((DF_HEADS, TQ, TK), const3),
            pl.BlockSpec((2, HEAD_DIM), const2),
            pl.BlockSpec((2, HEAD_DIM), const2),
            pl.BlockSpec((1, LANES), const2),
            pl.BlockSpec((SLAB, LANES), const2),
        ],
        out_specs=pl.BlockSpec((TQ, SLAB), lambda b, i: (b * nq + i, 0)),
        out_shape=jax.ShapeDtypeStruct((t, SLAB), BF16),
        scratch_shapes=[pltpu.VMEM((2 * DF_HEADS, TQ, 2 * LANES), F32),
                        pltpu.VMEM((1, LANES), F32)],
        compiler_params=pltpu.CompilerParams(
            dimension_semantics=("arbitrary", "arbitrary"), vmem_limit_bytes=VMEM_LIMIT),
        name="df_attention",
    )(slopes, proj, proj, proj, proj, bias, dbias, lq, lk, gain, gsum)


def _out_kernel(x_ref, asb_ref, adf_ref, gpre_ref, gpost_ref, wg_ref, bg_ref, wsb_ref,
                wdf_ref, wo_ref, o_ref, m_ref, *, col_chunk):
    x = x_ref[...]
    d = x.shape[1]
    h = _rms(x, gpre_ref[...]).astype(BF16)
    a_sb = asb_ref[...]
    a_df = adf_ref[...]
    for n in range(0, d, col_chunk):
        cols = slice(n, n + col_chunk)
        dcols = slice(d + n, d + n + col_chunk)
        g_sb = jax.nn.sigmoid(_dot(h, wg_ref[:, cols]) + bg_ref[:, cols])
        g_df = jax.nn.sigmoid(_dot(h, wg_ref[:, dcols]) + bg_ref[:, dcols])
        y_sb = _dot(a_sb, wsb_ref[:, cols])
        y_df = _dot(a_df, wdf_ref[:, cols])
        m_ref[:, cols] = (g_sb * y_sb + g_df * y_df).astype(BF16)
    out = _dot(m_ref[...], wo_ref[...])
    o_ref[...] = x + _rms(out, gpost_ref[...])


def _output_stage(xf, a_sb, a_df, gpre, gpost, wg, bg, wsb, wdf, wo):
    t, d = xf.shape
    const = lambda i: (0, 0)
    return pl.pallas_call(
        functools.partial(_out_kernel, col_chunk=256),
        grid=(t // TM_OUT,),
        in_specs=[
            pl.BlockSpec((TM_OUT, d), lambda i: (i, 0)),
            pl.BlockSpec((TM_OUT, SLAB), lambda i: (i, 0)),
            pl.BlockSpec((TM_OUT, SLAB), lambda i: (i, 0)),
            pl.BlockSpec((1, d), const),
            pl.BlockSpec((1, d), const),
            pl.BlockSpec(wg.shape, const),
            pl.BlockSpec(bg.shape, const),
            pl.BlockSpec(wsb.shape, const),
            pl.BlockSpec(wdf.shape, const),
            pl.BlockSpec(wo.shape, const),
        ],
        out_specs=pl.BlockSpec((TM_OUT, d), lambda i: (i, 0)),
        out_shape=jax.ShapeDtypeStruct((t, d), F32),
        scratch_shapes=[pltpu.VMEM((TM_OUT, d), BF16)],
        compiler_params=pltpu.CompilerParams(
            dimension_semantics=("arbitrary",), vmem_limit_bytes=VMEM_LIMIT),
        name="output_stage",
    )(xf, a_sb, a_df, gpre, gpost, wg, bg, wsb, wdf, wo)


def _constant_tiles():
    r = np.arange(TQ)[:, None]
    c = np.arange(TK)[None, :]
    tri = (r >= c).astype(np.float32)
    gsum = (np.arange(SLAB)[:, None] // HEAD_DIM == np.arange(LANES)[None, :]).astype(np.float32)
    slopes = np.array([2.0 ** (-8.0 * (h + 1) / DF_HEADS) for h in range(DF_HEADS)],
                      dtype=np.float64) * LOG2E
    rel = (r - c).astype(np.float64)
    bias = -slopes[:, None, None] * rel[None]
    visible = (c // CHUNK) <= (r // CHUNK)
    dbias = np.where(visible[None], -slopes[:, None, None] * np.abs(rel)[None], NEG_BIG)
    block_slopes = slopes * TQ
    return (jnp.asarray(tri, BF16), jnp.asarray(gsum, BF16), jnp.asarray(block_slopes, F32),
            jnp.asarray(bias, F32), jnp.asarray(dbias, F32))


def kernel(x, norm_pre, norm_post, w_in, w_gate, b_gate, w_branch_sb, w_branch_df, w_out,
           lambda_q, lambda_k, df_head_norm):
    b, s, d = x.shape
    depth = w_in.shape[0]
    assert s % TQ == 0 and TQ == TK and TQ % CHUNK == 0
    assert (b * s) % TM_PROJ == 0 and (b * s) % TM_OUT == 0
    tri, gsum, block_slopes, bias, dbias = _constant_tiles()
    xf = x.reshape(b * s, d)
    for l in range(depth):
        lambda_init = 0.8 - 0.6 * float(np.exp(-0.3 * l))
        proj = _in_projection(xf, norm_pre[l][None], w_in[l].astype(BF16))
        a_sb = _sb_attention(proj, tri, gsum, b, s)
        a_df = _df_attention(proj, block_slopes, bias, dbias, lambda_q[l].astype(F32),
                             lambda_k[l].astype(F32), df_head_norm[l][None].astype(F32),
                             gsum, b, s, lambda_init)
        xf = _output_stage(xf, a_sb, a_df, norm_pre[l][None], norm_post[l][None],
                           w_gate[l].astype(BF16), b_gate[l][None],
                           w_branch_sb[l].astype(BF16), w_branch_df[l].astype(BF16),
                           w_out[l].astype(BF16))
    return xf.reshape(b, s, d)
```

```python
import functools
import math

import numpy as np
import jax
import jax.numpy as jnp
from jax import lax
from jax.experimental import pallas as pl
from jax.experimental.pallas import tpu as pltpu

F32 = jnp.float32
BF16 = jnp.bfloat16

EPS = 1e-6
CHUNK = 64
HEAD_DIM = 64
LANES = 128
N_GROUPS = 4
SLAB = N_GROUPS * LANES
SB_HEADS = SLAB // HEAD_DIM
DF_HEADS = 4
LOG2E = math.log2(math.e)
Q_SCALE = HEAD_DIM ** -0.5 * LOG2E

TQ = 256
TK = 256
TM_PROJ = 512
TM_OUT = 512
NEG_BIG = -1e30
SKIP_LOG2 = 140.0
BOUND_MARGIN = 1.01
MAX_FIXED_REF_LOG2 = 100.0

VMEM_LIMIT = 56 * 1024 * 1024


def _nt_dot(a, b):
    return lax.dot_general(a, b, (((1,), (1,)), ((), ())), preferred_element_type=F32)


def _dot(a, b):
    return jnp.dot(a, b, preferred_element_type=F32)


def _rms(x, gain):
    return x * lax.rsqrt(jnp.mean(x * x, axis=-1, keepdims=True) + EPS) * gain


def _silu(z):
    return z / (1.0 + jnp.exp(-z))


def _lane_halves(x):
    lane = lax.broadcasted_iota(jnp.int32, (1, LANES), 1)
    first = lane < HEAD_DIM
    zero = jnp.zeros_like(x)
    return jnp.where(first, x, zero), jnp.where(first, zero, x)


def _max_sq_key_norms(k_ref, gsum_ref):
    kn = jnp.zeros((1, LANES), F32)
    for r in range(0, k_ref.shape[0], TK):
        kf = k_ref[r:r + TK, :].astype(F32)
        n2 = _dot((kf * kf).astype(BF16), gsum_ref[...])
        kn = jnp.maximum(kn, jnp.max(n2, axis=0, keepdims=True))
    return kn


def _logit_bounds(q_ref, kn_ref, gsum_ref):
    qf = q_ref[...].astype(F32)
    qn = jnp.max(_dot((qf * qf).astype(BF16), gsum_ref[...]), axis=0, keepdims=True)
    return jnp.sqrt(qn * kn_ref[...]) * BOUND_MARGIN


def _lane_pack(column_values):
    lane = lax.broadcasted_iota(jnp.int32, (1, LANES), 1)
    row = jnp.zeros((1, LANES), F32)
    for h, v in enumerate(column_values):
        row = jnp.where(lane == h, v, row)
    return row


def _lane_reduce(row, n, op, fill):
    lane = lax.broadcasted_iota(jnp.int32, (1, LANES), 1)
    return op(jnp.where(lane < n, row, fill))


def _proj_kernel(x_ref, g_ref, w_ref, o_ref, *, col_chunk, q_chunks):
    h = _rms(x_ref[...], g_ref[...]).astype(BF16)
    n_cols = o_ref.shape[1]
    for n in range(0, n_cols, col_chunk):
        y = _dot(h, w_ref[:, n:n + col_chunk])
        if n // col_chunk in q_chunks:
            y = y * Q_SCALE
        o_ref[:, n:n + col_chunk] = y.astype(BF16)


def _in_projection(xf, gain, w):
    t, d = xf.shape
    n = w.shape[1]
    return pl.pallas_call(
        functools.partial(_proj_kernel, col_chunk=SLAB, q_chunks=(0, 4)),
        grid=(t // TM_PROJ,),
        in_specs=[
            pl.BlockSpec((TM_PROJ, d), lambda i: (i, 0)),
            pl.BlockSpec((1, d), lambda i: (0, 0)),
            pl.BlockSpec((d, n), lambda i: (0, 0)),
        ],
        out_specs=pl.BlockSpec((TM_PROJ, n), lambda i: (i, 0)),
        out_shape=jax.ShapeDtypeStruct((t, n), BF16),
        compiler_params=pltpu.CompilerParams(
            dimension_semantics=("arbitrary",), vmem_limit_bytes=VMEM_LIMIT),
        name="in_projection",
    )(xf, gain, w)


def _softplus2(z2):
    return jnp.maximum(z2, 0.0) + jnp.log2(1.0 + jnp.exp2(jnp.minimum(z2, -z2)))


def _sb_kernel(q_ref, k_ref, v_ref, z_ref, tri2_ref, gsum_ref, o_ref, acc_ref, kn_ref):
    i = pl.program_id(1)

    @pl.when(i == 0)
    def _():
        kn_ref[...] = _max_sq_key_norms(k_ref, gsum_ref)

    zbound = _logit_bounds(q_ref, kn_ref, gsum_ref)

    q_heads = []
    for g in range(N_GROUPS):
        q_heads += list(_lane_halves(q_ref[:, g * LANES:(g + 1) * LANES]))
    tri2 = tri2_ref[...]

    row = lax.broadcasted_iota(jnp.int32, (TQ, TK), 0)
    col = lax.broadcasted_iota(jnp.int32, (TQ, TK), 1)
    strict = col < row

    def block(j, carries, diag):
        start = pl.multiple_of(j * TK, TK)
        new_carries, pvs = [], []
        for g in range(N_GROUPS):
            cols = slice(g * LANES, (g + 1) * LANES)
            kb = k_ref[pl.ds(start, TK), cols]
            v_heads = _lane_halves(v_ref[pl.ds(start, TK), cols])
            ws = []
            for sub in range(2):
                h = 2 * g + sub
                z = _nt_dot(q_heads[h], kb)
                sp = _softplus2(z)
                if diag:
                    sp = jnp.where(strict, sp, 0.0)
                hi = sp.astype(BF16)
                lo = (sp - hi.astype(F32)).astype(BF16)
                suffix = _dot(jnp.concatenate([hi, lo], axis=1), tri2)
                w = jnp.exp2(z - suffix - carries[h])
                if diag:
                    w = jnp.where(strict, w, 0.0)
                new_carries.append(carries[h] + jnp.sum(sp, axis=1, keepdims=True))
                ws.append(w.astype(BF16))
            pvs.append(_dot(jnp.concatenate(ws, axis=1), jnp.concatenate(v_heads, axis=0)))
        return tuple(new_carries), pvs

    def skip_slack(carries):
        least = _lane_pack([jnp.min(c, axis=0, keepdims=True) for c in carries])
        return _lane_reduce(least - zbound, SB_HEADS, jnp.min, jnp.inf)

    zeros = (jnp.zeros((TQ, 1), F32),) * SB_HEADS

    def diagonal_and_previous(_):
        carries, pv_diag = block(i, zeros, True)
        carries, pv_prev = block(i - 1, carries, False)
        for g in range(N_GROUPS):
            acc_ref[g] = pv_diag[g] + pv_prev[g]
        return (jnp.int32(1), skip_slack(carries)) + carries

    def diagonal_only(_):
        carries, pv_diag = block(i, zeros, True)
        for g in range(N_GROUPS):
            acc_ref[g] = pv_diag[g]
        return (jnp.int32(0), skip_slack(carries)) + carries

    state = lax.cond(i > 0, diagonal_and_previous, diagonal_only, 0)

    def cond(state):
        return jnp.logical_and(state[0] < i, state[1] < SKIP_LOG2)

    def body(state):
        t = state[0]
        carries, pvs = block(i - 1 - t, state[2:], False)
        for g in range(N_GROUPS):
            acc_ref[g] += pvs[g]
        return (t + 1, skip_slack(carries)) + carries

    lax.while_loop(cond, body, state)

    for g in range(N_GROUPS):
        cols = slice(g * LANES, (g + 1) * LANES)
        o_ref[:, cols] = (acc_ref[g] * _silu(z_ref[:, cols].astype(F32))).astype(BF16)


def _sb_attention(proj, tri2, gsum, batch, seq):
    nq = seq // TQ
    t = proj.shape[0]
    return pl.pallas_call(
        _sb_kernel,
        grid=(batch, nq),
        in_specs=[
            pl.BlockSpec((TQ, SLAB), lambda b, i: (b * nq + i, 0)),
            pl.BlockSpec((seq, SLAB), lambda b, i: (b, 1)),
            pl.BlockSpec((seq, SLAB), lambda b, i: (b, 2)),
            pl.BlockSpec((TQ, SLAB), lambda b, i: (b * nq + i, 3)),
            pl.BlockSpec((2 * TK, TK), lambda b, i: (0, 0)),
            pl.BlockSpec((SLAB, LANES), lambda b, i: (0, 0)),
        ],
        out_specs=pl.BlockSpec((TQ, SLAB), lambda b, i: (b * nq + i, 0)),
        out_shape=jax.ShapeDtypeStruct((t, SLAB), BF16),
        scratch_shapes=[pltpu.VMEM((N_GROUPS, TQ, LANES), F32),
                        pltpu.VMEM((1, LANES), F32)],
        compiler_params=pltpu.CompilerParams(
            dimension_semantics=("arbitrary", "arbitrary"), vmem_limit_bytes=VMEM_LIMIT),
        name="sb_attention",
    )(proj, proj, proj, proj, tri2, gsum)


def _df_kernel(slope_ref, q_ref, k_ref, v_ref, z_ref, bias_ref, dbias_ref, lq_ref, lk_ref,
               gain_ref, gsum_ref, o_ref, acc_ref, kn_ref, *, lambda_init):
    i = pl.program_id(1)
    n_chain = 2 * DF_HEADS

    @pl.when(i == 0)
    def _():
        kn_ref[...] = _max_sq_key_norms(k_ref, gsum_ref)

    lbound = _logit_bounds(q_ref, kn_ref, gsum_ref)
    q_comps = []
    for hd in range(DF_HEADS):
        q_comps += list(_lane_halves(q_ref[:, hd * LANES:(hd + 1) * LANES]))

    def scores(j, c, bias):
        hd = c // 2
        start = pl.multiple_of(j * TK, TK)
        return _nt_dot(q_comps[c], k_ref[pl.ds(start, TK), hd * LANES:(hd + 1) * LANES]) + bias

    ones = jnp.ones((TK, LANES), BF16)

    def values(j, hd):
        vb = v_ref[pl.ds(pl.multiple_of(j * TK, TK), TK), hd * LANES:(hd + 1) * LANES]
        return jnp.concatenate([vb, ones], axis=1)

    ms = []
    for c in range(n_chain):
        s = scores(i, c, dbias_ref[c // 2])
        m = jnp.max(s, axis=1, keepdims=True)
        ms.append(m)
        acc_ref[c] = _dot(jnp.exp2(s - m).astype(BF16), values(i, c // 2))
    least = _lane_pack([jnp.min(m, axis=0, keepdims=True) for m in ms])
    gap = _lane_reduce(lbound - least, n_chain, jnp.max, -jnp.inf)
    fixed_ref_ok = gap <= MAX_FIXED_REF_LOG2

    def block_shift(t, hd):
        return slope_ref[hd] * (t + 1).astype(F32)

    @pl.when(fixed_ref_ok)
    def _():
        def add_blocks(ts):
            for c in range(n_chain):
                hd = c // 2
                ps = [jnp.exp2(scores(i - 1 - t, c, bias_ref[hd]) - (ms[c] + block_shift(t, hd)))
                      .astype(BF16) for t in ts]
                vs = [values(i - 1 - t, hd) for t in ts]
                acc_ref[c] += _dot(jnp.concatenate(ps, axis=1), jnp.concatenate(vs, axis=0))

        def pair(u, carry):
            add_blocks((2 * u, 2 * u + 1))
            return carry

        lax.fori_loop(0, lax.shift_right_logical(i, 1), pair, 0)

        @pl.when(lax.bitwise_and(i, 1) == 1)
        def _():
            add_blocks((i - 1,))

    @pl.when(jnp.logical_not(fixed_ref_ok))
    def _():
        def body(t, m_run):
            j = i - 1 - t
            m_out = []
            for c in range(n_chain):
                hd = c // 2
                shift = block_shift(t, hd)
                s = scores(j, c, bias_ref[hd])
                m_new = jnp.maximum(m_run[c], jnp.max(s, axis=1, keepdims=True) - shift)
                alpha = jnp.exp2(m_run[c] - m_new)
                p = jnp.exp2(s - (m_new + shift))
                m_out.append(m_new)
                acc_ref[c] = alpha * acc_ref[c] + _dot(p.astype(BF16), values(j, hd))
            return tuple(m_out)

        lax.fori_loop(0, i, body, tuple(ms))

    dots = jnp.sum(lq_ref[...] * lk_ref[...], axis=1, keepdims=True)
    lam = jnp.exp(dots[0:1]) - jnp.exp(dots[1:2]) + lambda_init
    for hd in range(DF_HEADS):
        cols = slice(hd * LANES, (hd + 1) * LANES)
        a0, a1 = acc_ref[2 * hd], acc_ref[2 * hd + 1]
        o = a0[:, :LANES] / a0[:, LANES:] - lam * (a1[:, :LANES] / a1[:, LANES:])
        o = _rms(o, gain_ref[...]) * (1.0 - lambda_init)
        o_ref[:, cols] = (o * _silu(z_ref[:, cols].astype(F32))).astype(BF16)


def _df_attention(proj, slopes, bias, dbias, lq, lk, gain, gsum, batch, seq, lambda_init):
    nq = seq // TQ
    t = proj.shape[0]
    const3 = lambda b, i: (0, 0, 0)
    const2 = lambda b, i: (0, 0)
    return pl.pallas_call(
        functools.partial(_df_kernel, lambda_init=lambda_init),
        grid=(batch, nq),
        in_specs=[
            pl.BlockSpec(memory_space=pltpu.SMEM),
            pl.BlockSpec((TQ, SLAB), lambda b, i: (b * nq + i, 4)),
            pl.BlockSpec((seq, SLAB), lambda b, i: (b, 5)),
            pl.BlockSpec((seq, SLAB), lambda b, i: (b, 6)),
            pl.BlockSpec((TQ, SLAB), lambda b, i: (b * nq + i, 7)),
            pl.BlockSpec((DF_HEADS, TQ, TK), const3),
            pl.BlockSpec((DF_HEADS, TQ, TK), const3),
            pl.BlockSpec((2, HEAD_DIM), const2),
            pl.BlockSpec((2, HEAD_DIM), const2),
            pl.BlockSpec((1, LANES), const2),
            pl.BlockSpec((SLAB, LANES), const2),
        ],
        out_specs=pl.BlockSpec((TQ, SLAB), lambda b, i: (b * nq + i, 0)),
        out_shape=jax.ShapeDtypeStruct((t, SLAB), BF16),
        scratch_shapes=[pltpu.VMEM((2 * DF_HEADS, TQ, 2 * LANES), F32),
                        pltpu.VMEM((1, LANES), F32)],
        compiler_params=pltpu.CompilerParams(
            dimension_semantics=("arbitrary", "arbitrary"), vmem_limit_bytes=VMEM_LIMIT),
        name="df_attention",
    )(slopes, proj, proj, proj, proj, bias, dbias, lq, lk, gain, gsum)


def _out_kernel(x_ref, asb_ref, adf_ref, gpre_ref, gpost_ref, wg_ref, bg_ref, wsb_ref,
                wdf_ref, wo_ref, o_ref, m_ref, *, col_chunk):
    x = x_ref[...]
    d = x.shape[1]
    h = _rms(x, gpre_ref[...]).astype(BF16)
    a_sb = asb_ref[...]
    a_df = adf_ref[...]
    for n in range(0, d, col_chunk):
        cols = slice(n, n + col_chunk)
        dcols = slice(d + n, d + n + col_chunk)
        g_sb = jax.nn.sigmoid(_dot(h, wg_ref[:, cols]) + bg_ref[:, cols])
        g_df = jax.nn.sigmoid(_dot(h, wg_ref[:, dcols]) + bg_ref[:, dcols])
        y_sb = _dot(a_sb, wsb_ref[:, cols])
        y_df = _dot(a_df, wdf_ref[:, cols])
        m_ref[:, cols] = (g_sb * y_sb + g_df * y_df).astype(BF16)
    out = _dot(m_ref[...], wo_ref[...])
    o_ref[...] = x + _rms(out, gpost_ref[...])


def _output_stage(xf, a_sb, a_df, gpre, gpost, wg, bg, wsb, wdf, wo):
    t, d = xf.shape
    const = lambda i: (0, 0)
    return pl.pallas_call(
        functools.partial(_out_kernel, col_chunk=256),
        grid=(t // TM_OUT,),
        in_specs=[
            pl.BlockSpec((TM_OUT, d), lambda i: (i, 0)),
            pl.BlockSpec((TM_OUT, SLAB), lambda i: (i, 0)),
            pl.BlockSpec((TM_OUT, SLAB), lambda i: (i, 0)),
            pl.BlockSpec((1, d), const),
            pl.BlockSpec((1, d), const),
            pl.BlockSpec(wg.shape, const),
            pl.BlockSpec(bg.shape, const),
            pl.BlockSpec(wsb.shape, const),
            pl.BlockSpec(wdf.shape, const),
            pl.BlockSpec(wo.shape, const),
        ],
        out_specs=pl.BlockSpec((TM_OUT, d), lambda i: (i, 0)),
        out_shape=jax.ShapeDtypeStruct((t, d), F32),
        scratch_shapes=[pltpu.VMEM((TM_OUT, d), BF16)],
        compiler_params=pltpu.CompilerParams(
            dimension_semantics=("arbitrary",), vmem_limit_bytes=VMEM_LIMIT),
        name="output_stage",
    )(xf, a_sb, a_df, gpre, gpost, wg, bg, wsb, wdf, wo)


def _constant_tiles():
    r = np.arange(TQ)[:, None]
    c = np.arange(TK)[None, :]
    tri = (r >= c).astype(np.float32)
    tri2 = np.concatenate([tri, tri], axis=0)
    gsum = (np.arange(SLAB)[:, None] // HEAD_DIM == np.arange(LANES)[None, :]).astype(np.float32)
    slopes = np.array([2.0 ** (-8.0 * (h + 1) / DF_HEADS) for h in range(DF_HEADS)],
                      dtype=np.float64) * LOG2E
    rel = (r - c).astype(np.float64)
    bias = -slopes[:, None, None] * rel[None]
    visible = (c // CHUNK) <= (r // CHUNK)
    dbias = np.where(visible[None], -slopes[:, None, None] * np.abs(rel)[None], NEG_BIG)
    block_slopes = slopes * TQ
    return (jnp.asarray(tri2, BF16), jnp.asarray(gsum, BF16), jnp.asarray(block_slopes, F32),
            jnp.asarray(bias, F32), jnp.asarray(dbias, F32))


def kernel(x, norm_pre, norm_post, w_in, w_gate, b_gate, w_branch_sb, w_branch_df, w_out,
           lambda_q, lambda_k, df_head_norm):
    b, s, d = x.shape
    depth = w_in.shape[0]
    assert s % TQ == 0 and TQ == TK and TQ % CHUNK == 0
    assert (b * s) % TM_PROJ == 0 and (b * s) % TM_OUT == 0
    tri2, gsum, block_slopes, bias, dbias = _constant_tiles()
    xf = x.reshape(b * s, d)
    for l in range(depth):
        lambda_init = 0.8 - 0.6 * float(np.exp(-0.3 * l))
        proj = _in_projection(xf, norm_pre[l][None], w_in[l].astype(BF16))
        a_sb = _sb_attention(proj, tri2, gsum, b, s)
        a_df = _df_attention(proj, block_slopes, bias, dbias, lambda_q[l].astype(F32),
                             lambda_k[l].astype(F32), df_head_norm[l][None].astype(F32),
                             gsum, b, s, lambda_init)
        xf = _output_stage(xf, a_sb, a_df, norm_pre[l][None], norm_post[l][None],
                           w_gate[l].astype(BF16), b_gate[l][None],
                           w_branch_sb[l].astype(BF16), w_branch_df[l].astype(BF16),
                           w_out[l].astype(BF16))
    return xf.reshape(b, s, d)
```

```python
import functools
import math

import numpy as np
import jax
import jax.numpy as jnp
from jax import lax
from jax.experimental import pallas as pl
from jax.experimental.pallas import tpu as pltpu

F32 = jnp.float32
BF16 = jnp.bfloat16

EPS = 1e-6
CHUNK = 64
HEAD_DIM = 64
LANES = 128
N_GROUPS = 4
SLAB = N_GROUPS * LANES
SB_HEADS = SLAB // HEAD_DIM
DF_HEADS = 4
LOG2E = math.log2(math.e)
Q_SCALE = HEAD_DIM ** -0.5 * LOG2E

TQ = 256
TK = 256
TM_PROJ = 1024
TM_OUT = 1024
NEG_BIG = -1e30
SKIP_LOG2 = 140.0
BOUND_MARGIN = 1.01
MAX_FIXED_REF_LOG2 = 100.0
SOFTPLUS_CLAMP_LOG2 = 100.0

VMEM_LIMIT = 56 * 1024 * 1024


def _nt_dot(a, b):
    return lax.dot_general(a, b, (((1,), (1,)), ((), ())), preferred_element_type=F32)


def _dot(a, b):
    return jnp.dot(a, b, preferred_element_type=F32)


def _rms(x, gain):
    return x * lax.rsqrt(jnp.mean(x * x, axis=-1, keepdims=True) + EPS) * gain


def _silu(z):
    return z / (1.0 + jnp.exp(-z))


def _lane_halves(x):
    lane = lax.broadcasted_iota(jnp.int32, (1, LANES), 1)
    first = lane < HEAD_DIM
    zero = jnp.zeros_like(x)
    return jnp.where(first, x, zero), jnp.where(first, zero, x)


def _max_sq_key_norms(k_ref, gsum_ref):
    kn = jnp.zeros((1, LANES), F32)
    for r in range(0, k_ref.shape[0], TK):
        kf = k_ref[r:r + TK, :].astype(F32)
        n2 = _dot((kf * kf).astype(BF16), gsum_ref[...])
        kn = jnp.maximum(kn, jnp.max(n2, axis=0, keepdims=True))
    return kn


def _logit_bounds(q_ref, kn_ref, gsum_ref):
    qf = q_ref[...].astype(F32)
    qn = jnp.max(_dot((qf * qf).astype(BF16), gsum_ref[...]), axis=0, keepdims=True)
    return jnp.sqrt(qn * kn_ref[...]) * BOUND_MARGIN


def _lane_pack(column_values):
    lane = lax.broadcasted_iota(jnp.int32, (1, LANES), 1)
    row = jnp.zeros((1, LANES), F32)
    for h, v in enumerate(column_values):
        row = jnp.where(lane == h, v, row)
    return row


def _lane_reduce(row, n, op, fill):
    lane = lax.broadcasted_iota(jnp.int32, (1, LANES), 1)
    return op(jnp.where(lane < n, row, fill))


def _proj_kernel(x_ref, g_ref, w_ref, o_ref, *, col_chunk, q_chunks, row_block):
    n_cols = o_ref.shape[1]
    for r in range(0, x_ref.shape[0], row_block):
        rows = slice(r, r + row_block)
        h = _rms(x_ref[rows, :], g_ref[...]).astype(BF16)
        for n in range(0, n_cols, col_chunk):
            y = _dot(h, w_ref[:, n:n + col_chunk])
            if n // col_chunk in q_chunks:
                y = y * Q_SCALE
            o_ref[rows, n:n + col_chunk] = y.astype(BF16)


def _in_projection(xf, gain, w):
    t, d = xf.shape
    n = w.shape[1]
    return pl.pallas_call(
        functools.partial(_proj_kernel, col_chunk=SLAB, q_chunks=(0, 4), row_block=512),
        grid=(t // TM_PROJ,),
        in_specs=[
            pl.BlockSpec((TM_PROJ, d), lambda i: (i, 0)),
            pl.BlockSpec((1, d), lambda i: (0, 0)),
            pl.BlockSpec((d, n), lambda i: (0, 0), pipeline_mode=pl.Buffered(1)),
        ],
        out_specs=pl.BlockSpec((TM_PROJ, n), lambda i: (i, 0)),
        out_shape=jax.ShapeDtypeStruct((t, n), BF16),
        compiler_params=pltpu.CompilerParams(
            dimension_semantics=("arbitrary",), vmem_limit_bytes=VMEM_LIMIT),
        name="in_projection",
    )(xf, gain, w)


def _softplus2(z2):
    return jnp.maximum(z2, jnp.log2(1.0 + jnp.exp2(jnp.minimum(z2, SOFTPLUS_CLAMP_LOG2))))


def _sb_kernel(q_ref, k_ref, v_ref, tri2_ref, gsum_ref, o_ref, kn_ref):
    i = pl.program_id(1)

    @pl.when(i == 0)
    def _():
        kn_ref[...] = _max_sq_key_norms(k_ref, gsum_ref)

    zbound = _logit_bounds(q_ref, kn_ref, gsum_ref)

    q_heads = []
    for g in range(N_GROUPS):
        q_heads += list(_lane_halves(q_ref[:, g * LANES:(g + 1) * LANES]))
    tri2 = tri2_ref[...]

    row = lax.broadcasted_iota(jnp.int32, (TQ, TK), 0)
    col = lax.broadcasted_iota(jnp.int32, (TQ, TK), 1)
    strict = col < row

    def block(j, carries, diag):
        start = pl.multiple_of(j * TK, TK)
        new_carries, pvs = [], []
        for g in range(N_GROUPS):
            cols = slice(g * LANES, (g + 1) * LANES)
            kb = k_ref[pl.ds(start, TK), cols]
            v_heads = _lane_halves(v_ref[pl.ds(start, TK), cols])
            ws = []
            for sub in range(2):
                h = 2 * g + sub
                z = _nt_dot(q_heads[h], kb)
                sp = _softplus2(z)
                if diag:
                    sp = jnp.where(strict, sp, 0.0)
                hi = sp.astype(BF16)
                lo = (sp - hi.astype(F32)).astype(BF16)
                suffix = _dot(jnp.concatenate([hi, lo], axis=1), tri2)
                w = jnp.exp2(z - suffix - carries[h])
                if diag:
                    w = jnp.where(strict, w, 0.0)
                new_carries.append(carries[h] + jnp.sum(sp, axis=1, keepdims=True))
                ws.append(w.astype(BF16))
            pvs.append(_dot(jnp.concatenate(ws, axis=1), jnp.concatenate(v_heads, axis=0)))
        return tuple(new_carries), pvs

    def skip_slack(carries):
        least = _lane_pack([jnp.min(c, axis=0, keepdims=True) for c in carries])
        return _lane_reduce(least - zbound, SB_HEADS, jnp.min, jnp.inf)

    zeros = (jnp.zeros((TQ, 1), F32),) * SB_HEADS

    def diagonal_and_previous(_):
        carries, pv_diag = block(i, zeros, True)
        carries, pv_prev = block(i - 1, carries, False)
        for g in range(N_GROUPS):
            o_ref[:, g * LANES:(g + 1) * LANES] = pv_diag[g] + pv_prev[g]
        return (jnp.int32(1), skip_slack(carries)) + carries

    def diagonal_only(_):
        carries, pv_diag = block(i, zeros, True)
        for g in range(N_GROUPS):
            o_ref[:, g * LANES:(g + 1) * LANES] = pv_diag[g]
        return (jnp.int32(0), skip_slack(carries)) + carries

    state = lax.cond(i > 0, diagonal_and_previous, diagonal_only, 0)

    def cond(state):
        return jnp.logical_and(state[0] < i, state[1] < SKIP_LOG2)

    def body(state):
        t = state[0]
        carries, pvs = block(i - 1 - t, state[2:], False)
        for g in range(N_GROUPS):
            o_ref[:, g * LANES:(g + 1) * LANES] += pvs[g]
        return (t + 1, skip_slack(carries)) + carries

    lax.while_loop(cond, body, state)


def _sb_attention(proj, tri2, gsum, batch, seq):
    nq = seq // TQ
    t = proj.shape[0]
    return pl.pallas_call(
        _sb_kernel,
        grid=(batch, nq),
        in_specs=[
            pl.BlockSpec((TQ, SLAB), lambda b, i: (b * nq + i, 0)),
            pl.BlockSpec((seq, SLAB), lambda b, i: (b, 1)),
            pl.BlockSpec((seq, SLAB), lambda b, i: (b, 2)),
            pl.BlockSpec((2 * TK, TK), lambda b, i: (0, 0)),
            pl.BlockSpec((SLAB, LANES), lambda b, i: (0, 0)),
        ],
        out_specs=pl.BlockSpec((TQ, SLAB), lambda b, i: (b * nq + i, 0)),
        out_shape=jax.ShapeDtypeStruct((t, SLAB), F32),
        scratch_shapes=[pltpu.VMEM((1, LANES), F32)],
        compiler_params=pltpu.CompilerParams(
            dimension_semantics=("arbitrary", "arbitrary"), vmem_limit_bytes=VMEM_LIMIT),
        name="sb_attention",
    )(proj, proj, proj, tri2, gsum)


def _df_kernel(slope_ref, q_ref, k_ref, v_ref, bias_ref, dbias_ref, lq_ref, lk_ref,
               gsum_ref, o_ref, acc_ref, kn_ref, *, lambda_init):
    i = pl.program_id(1)
    n_chain = 2 * DF_HEADS

    @pl.when(i == 0)
    def _():
        kn_ref[...] = _max_sq_key_norms(k_ref, gsum_ref)

    lbound = _logit_bounds(q_ref, kn_ref, gsum_ref)
    fixed_ref_ok = 2.0 * _lane_reduce(lbound, n_chain, jnp.max, -jnp.inf) <= MAX_FIXED_REF_LOG2
    reach = (2.0 * _lane_reduce(lbound, 2, jnp.max, -jnp.inf) + SKIP_LOG2) / slope_ref[0]
    n_near = jnp.int32(1)
    for t in range(1, k_ref.shape[0] // TK):
        n_near += (reach >= t).astype(jnp.int32)
    n_near = jnp.minimum(n_near, i)

    q_comps = []
    for hd in range(DF_HEADS):
        q_comps += list(_lane_halves(q_ref[:, hd * LANES:(hd + 1) * LANES]))

    def scores(j, c, bias):
        hd = c // 2
        start = pl.multiple_of(j * TK, TK)
        return _nt_dot(q_comps[c], k_ref[pl.ds(start, TK), hd * LANES:(hd + 1) * LANES]) + bias

    ones = jnp.ones((TK, LANES), BF16)

    def values(j, hd):
        vb = v_ref[pl.ds(pl.multiple_of(j * TK, TK), TK), hd * LANES:(hd + 1) * LANES]
        return jnp.concatenate([vb, ones], axis=1)

    def block_shift(t, hd):
        return slope_ref[hd] * jnp.asarray(t + 1).astype(F32)

    def diagonal(c):
        s = scores(i, c, dbias_ref[c // 2])
        m = jnp.max(s, axis=1, keepdims=True)
        return m, jnp.exp2(s - m).astype(BF16)

    def earlier(t, c, m):
        hd = c // 2
        return jnp.exp2(scores(i - 1 - t, c, bias_ref[hd]) - (m + block_shift(t, hd))).astype(BF16)

    def contract(ps, js, hd):
        return _dot(jnp.concatenate(ps, axis=1), jnp.concatenate([values(j, hd) for j in js], axis=0))

    @pl.when(jnp.logical_and(fixed_ref_ok, i == 0))
    def _():
        for c in range(n_chain):
            acc_ref[c] = contract([diagonal(c)[1]], [i], c // 2)

    @pl.when(jnp.logical_and(fixed_ref_ok, i > 0))
    def _():
        ms = []
        for c in range(n_chain):
            m, p = diagonal(c)
            ms.append(m)
            acc_ref[c] = contract([p, earlier(0, c, m)], [i, i - 1], c // 2)

        def add_blocks(ts, chains):
            for c in chains:
                ps = [earlier(t, c, ms[c]) for t in ts]
                acc_ref[c] += contract(ps, [i - 1 - t for t in ts], c // 2)

        def near(t, carry):
            add_blocks((t,), range(n_chain))
            return carry

        lax.fori_loop(1, n_near, near, 0)

        far_chains = range(2, n_chain)
        n_far = i - n_near

        def far_pair(u, carry):
            add_blocks((n_near + 2 * u, n_near + 2 * u + 1), far_chains)
            return carry

        lax.fori_loop(0, lax.shift_right_logical(n_far, 1), far_pair, 0)

        @pl.when(lax.bitwise_and(n_far, 1) == 1)
        def _():
            add_blocks((i - 1,), far_chains)

    @pl.when(jnp.logical_not(fixed_ref_ok))
    def _():
        ms = []
        for c in range(n_chain):
            m, p = diagonal(c)
            ms.append(m)
            acc_ref[c] = contract([p], [i], c // 2)

        def body(t, m_run):
            j = i - 1 - t
            m_out = []
            for c in range(n_chain):
                hd = c // 2
                shift = block_shift(t, hd)
                s = scores(j, c, bias_ref[hd])
                m_new = jnp.maximum(m_run[c], jnp.max(s, axis=1, keepdims=True) - shift)
                alpha = jnp.exp2(m_run[c] - m_new)
                p = jnp.exp2(s - (m_new + shift))
                m_out.append(m_new)
                acc_ref[c] = alpha * acc_ref[c] + _dot(p.astype(BF16), values(j, hd))
            return tuple(m_out)

        lax.fori_loop(0, i, body, tuple(ms))

    dots = jnp.sum(lq_ref[...] * lk_ref[...], axis=1, keepdims=True)
    lam = jnp.exp(dots[0:1]) - jnp.exp(dots[1:2]) + lambda_init
    for hd in range(DF_HEADS):
        a0, a1 = acc_ref[2 * hd], acc_ref[2 * hd + 1]
        o_ref[:, hd * LANES:(hd + 1) * LANES] = (
            a0[:, :LANES] / a0[:, LANES:] - lam * (a1[:, :LANES] / a1[:, LANES:]))


def _df_attention(proj, slopes, bias, dbias, lq, lk, gsum, batch, seq, lambda_init):
    nq = seq // TQ
    t = proj.shape[0]
    const3 = lambda b, i: (0, 0, 0)
    const2 = lambda b, i: (0, 0)
    return pl.pallas_call(
        functools.partial(_df_kernel, lambda_init=lambda_init),
        grid=(batch, nq),
        in_specs=[
            pl.BlockSpec(memory_space=pltpu.SMEM),
            pl.BlockSpec((TQ, SLAB), lambda b, i: (b * nq + i, 4)),
            pl.BlockSpec((seq, SLAB), lambda b, i: (b, 5)),
            pl.BlockSpec((seq, SLAB), lambda b, i: (b, 6)),
            pl.BlockSpec((DF_HEADS, TQ, TK), const3),
            pl.BlockSpec((DF_HEADS, TQ, TK), const3),
            pl.BlockSpec((2, HEAD_DIM), const2),
            pl.BlockSpec((2, HEAD_DIM), const2),
            pl.BlockSpec((SLAB, LANES), const2),
        ],
        out_specs=pl.BlockSpec((TQ, SLAB), lambda b, i: (b * nq + i, 0)),
        out_shape=jax.ShapeDtypeStruct((t, SLAB), F32),
        scratch_shapes=[pltpu.VMEM((2 * DF_HEADS, TQ, 2 * LANES), F32),
                        pltpu.VMEM((1, LANES), F32)],
        compiler_params=pltpu.CompilerParams(
            dimension_semantics=("arbitrary", "arbitrary"), vmem_limit_bytes=VMEM_LIMIT),
        name="df_attention",
    )(slopes, proj, proj, proj, bias, dbias, lq, lk, gsum)


def _out_kernel(x_ref, osb_ref, odf_ref, zsb_ref, zdf_ref, gpre_ref, gpost_ref, ghead_ref,
                wg_ref, bg_ref, wsb_ref, wdf_ref, wo_ref, o_ref, m_ref,
                *, col_chunk, row_block, lambda_init):
    d = x_ref.shape[1]
    for r in range(0, x_ref.shape[0], row_block):
        rows = slice(r, r + row_block)
        x = x_ref[rows, :]
        h = _rms(x, gpre_ref[...]).astype(BF16)
        a_sb = (osb_ref[rows, :] * _silu(zsb_ref[rows, :].astype(F32))).astype(BF16)
        a_df = []
        for hd in range(DF_HEADS):
            cols = slice(hd * LANES, (hd + 1) * LANES)
            o = _rms(odf_ref[rows, cols], ghead_ref[...]) * (1.0 - lambda_init)
            a_df.append((o * _silu(zdf_ref[rows, cols].astype(F32))).astype(BF16))
        a_df = jnp.concatenate(a_df, axis=1)
        for n in range(0, d, col_chunk):
            cols = slice(n, n + col_chunk)
            dcols = slice(d + n, d + n + col_chunk)
            g_sb = jax.nn.sigmoid(_dot(h, wg_ref[:, cols]) + bg_ref[:, cols])
            g_df = jax.nn.sigmoid(_dot(h, wg_ref[:, dcols]) + bg_ref[:, dcols])
            y_sb = _dot(a_sb, wsb_ref[:, cols])
            y_df = _dot(a_df, wdf_ref[:, cols])
            m_ref[rows, cols] = (g_sb * y_sb + g_df * y_df).astype(BF16)
        out = _dot(m_ref[rows, :], wo_ref[...])
        o_ref[rows, :] = x + _rms(out, gpost_ref[...])


def _output_stage(xf, o_sb, o_df, proj, gpre, gpost, ghead, wg, bg, wsb, wdf, wo, lambda_init):
    t, d = xf.shape
    const = lambda i: (0, 0)
    resident = lambda a: pl.BlockSpec(a.shape, const, pipeline_mode=pl.Buffered(1))
    rows = lambda width, col: pl.BlockSpec((TM_OUT, width), lambda i: (i, col))
    return pl.pallas_call(
        functools.partial(_out_kernel, col_chunk=512, row_block=256, lambda_init=lambda_init),
        grid=(t // TM_OUT,),
        in_specs=[
            rows(d, 0),
            rows(SLAB, 0),
            rows(SLAB, 0),
            rows(SLAB, 3),
            rows(SLAB, 7),
            pl.BlockSpec((1, d), const),
            pl.BlockSpec((1, d), const),
            pl.BlockSpec((1, LANES), const),
            resident(wg), pl.BlockSpec(bg.shape, const), resident(wsb), resident(wdf), resident(wo),
        ],
        out_specs=rows(d, 0),
        out_shape=jax.ShapeDtypeStruct((t, d), F32),
        scratch_shapes=[pltpu.VMEM((TM_OUT, d), BF16)],
        compiler_params=pltpu.CompilerParams(
            dimension_semantics=("arbitrary",), vmem_limit_bytes=VMEM_LIMIT),
        name="output_stage",
    )(xf, o_sb, o_df, proj, proj, gpre, gpost, ghead, wg, bg, wsb, wdf, wo)


def _constant_tiles():
    r = np.arange(TQ)[:, None]
    c = np.arange(TK)[None, :]
    tri = (r >= c).astype(np.float32)
    tri2 = np.concatenate([tri, tri], axis=0)
    gsum = (np.arange(SLAB)[:, None] // HEAD_DIM == np.arange(LANES)[None, :]).astype(np.float32)
    slopes = np.array([2.0 ** (-8.0 * (h + 1) / DF_HEADS) for h in range(DF_HEADS)],
                      dtype=np.float64) * LOG2E
    rel = (r - c).astype(np.float64)
    bias = -slopes[:, None, None] * rel[None]
    visible = (c // CHUNK) <= (r // CHUNK)
    dbias = np.where(visible[None], -slopes[:, None, None] * np.abs(rel)[None], NEG_BIG)
    block_slopes = slopes * TQ
    return (jnp.asarray(tri2, BF16), jnp.asarray(gsum, BF16), jnp.asarray(block_slopes, F32),
            jnp.asarray(bias, F32), jnp.asarray(dbias, F32))


def kernel(x, norm_pre, norm_post, w_in, w_gate, b_gate, w_branch_sb, w_branch_df, w_out,
           lambda_q, lambda_k, df_head_norm):
    b, s, d = x.shape
    depth = w_in.shape[0]
    assert s % TQ == 0 and TQ == TK and TQ % CHUNK == 0
    assert (b * s) % TM_PROJ == 0 and (b * s) % TM_OUT == 0
    tri2, gsum, block_slopes, bias, dbias = _constant_tiles()
    xf = x.reshape(b * s, d)
    for l in range(depth):
        lambda_init = 0.8 - 0.6 * float(np.exp(-0.3 * l))
        proj = _in_projection(xf, norm_pre[l][None], w_in[l].astype(BF16))
        o_sb = _sb_attention(proj, tri2, gsum, b, s)
        o_df = _df_attention(proj, block_slopes, bias, dbias, lambda_q[l].astype(F32),
                             lambda_k[l].astype(F32), gsum, b, s, lambda_init)
        xf = _output_stage(xf, o_sb, o_df, proj, norm_pre[l][None], norm_post[l][None],
                           df_head_norm[l][None].astype(F32),
                           w_gate[l].astype(BF16), b_gate[l][None],
                           w_branch_sb[l].astype(BF16), w_branch_df[l].astype(BF16),
                           w_out[l].astype(BF16), lambda_init)
    return xf.reshape(b, s, d)
```

```python
import functools
import math

import numpy as np
import jax
import jax.numpy as jnp
from jax import lax
from jax.experimental import pallas as pl
from jax.experimental.pallas import tpu as pltpu

F32 = jnp.float32
BF16 = jnp.bfloat16

EPS = 1e-6
CHUNK = 64
HEAD_DIM = 64
LANES = 128
N_GROUPS = 4
SLAB = N_GROUPS * LANES
SB_HEADS = SLAB // HEAD_DIM
DF_HEADS = 4
LOG2E = math.log2(math.e)
Q_SCALE = HEAD_DIM ** -0.5 * LOG2E

TQ = 256
TK = 256
TM_PROJ = 1024
TM_OUT = 1024
NEG_BIG = -1e30
SKIP_LOG2 = 135.0
BOUND_MARGIN = 1.01
MAX_FIXED_REF_LOG2 = 100.0
SOFTPLUS_CLAMP_LOG2 = 100.0

VMEM_LIMIT = 56 * 1024 * 1024


def _nt_dot(a, b):
    return lax.dot_general(a, b, (((1,), (1,)), ((), ())), preferred_element_type=F32)


def _dot(a, b):
    return jnp.dot(a, b, preferred_element_type=F32)


def _rms(x, gain):
    return x * lax.rsqrt(jnp.mean(x * x, axis=-1, keepdims=True) + EPS) * gain


def _silu(z):
    return z / (1.0 + jnp.exp(-z))


def _lane_halves(x):
    lane = lax.broadcasted_iota(jnp.int32, (1, LANES), 1)
    first = lane < HEAD_DIM
    zero = jnp.zeros_like(x)
    return jnp.where(first, x, zero), jnp.where(first, zero, x)


def _max_sq_group_norms(ref, gsum_ref):
    best = jnp.zeros((1, LANES), F32)
    for r in range(0, ref.shape[0], TK):
        xf = ref[r:r + TK, :].astype(F32)
        n2 = _dot((xf * xf).astype(BF16), gsum_ref[...])
        best = jnp.maximum(best, jnp.max(n2, axis=0, keepdims=True))
    return best


def _logit_bounds(qall_ref, k_ref, gsum_ref):
    qn = _max_sq_group_norms(qall_ref, gsum_ref)
    kn = _max_sq_group_norms(k_ref, gsum_ref)
    return jnp.sqrt(qn * kn) * BOUND_MARGIN


def _lane_pack(column_values):
    lane = lax.broadcasted_iota(jnp.int32, (1, LANES), 1)
    row = jnp.zeros((1, LANES), F32)
    for h, v in enumerate(column_values):
        row = jnp.where(lane == h, v, row)
    return row


def _lane_reduce(row, n, op, fill):
    lane = lax.broadcasted_iota(jnp.int32, (1, LANES), 1)
    return op(jnp.where(lane < n, row, fill))


def _proj_kernel(x_ref, g_ref, w_ref, o_ref, *, col_chunk, q_chunks, row_block):
    n_cols = o_ref.shape[1]
    for r in range(0, x_ref.shape[0], row_block):
        rows = slice(r, r + row_block)
        h = _rms(x_ref[rows, :], g_ref[...]).astype(BF16)
        for n in range(0, n_cols, col_chunk):
            y = _dot(h, w_ref[:, n:n + col_chunk])
            if n // col_chunk in q_chunks:
                y = y * Q_SCALE
            o_ref[rows, n:n + col_chunk] = y.astype(BF16)


def _in_projection(xf, gain, w):
    t, d = xf.shape
    n = w.shape[1]
    return pl.pallas_call(
        functools.partial(_proj_kernel, col_chunk=SLAB, q_chunks=(0, 4), row_block=512),
        grid=(t // TM_PROJ,),
        in_specs=[
            pl.BlockSpec((TM_PROJ, d), lambda i: (i, 0)),
            pl.BlockSpec((1, d), lambda i: (0, 0)),
            pl.BlockSpec((d, n), lambda i: (0, 0), pipeline_mode=pl.Buffered(1)),
        ],
        out_specs=pl.BlockSpec((TM_PROJ, n), lambda i: (i, 0)),
        out_shape=jax.ShapeDtypeStruct((t, n), BF16),
        compiler_params=pltpu.CompilerParams(
            dimension_semantics=("arbitrary",), vmem_limit_bytes=VMEM_LIMIT),
        name="in_projection",
    )(xf, gain, w)


def _softplus2(z2):
    return jnp.maximum(z2, jnp.log2(1.0 + jnp.exp2(jnp.minimum(z2, SOFTPLUS_CLAMP_LOG2))))


def _sb_kernel(q_ref, k_ref, v_ref, tri2_ref, gsum_ref, o_ref, zb_ref):
    i = pl.program_id(1)

    @pl.when(i == 0)
    def _():
        zb_ref[...] = _logit_bounds(q_ref, k_ref, gsum_ref)

    zbound = zb_ref[...]

    q_rows = pl.ds(pl.multiple_of(i * TQ, TQ), TQ)
    q_heads = []
    for g in range(N_GROUPS):
        q_heads += list(_lane_halves(q_ref[q_rows, g * LANES:(g + 1) * LANES]))
    tri2 = tri2_ref[...]

    row = lax.broadcasted_iota(jnp.int32, (TQ, TK), 0)
    col = lax.broadcasted_iota(jnp.int32, (TQ, TK), 1)
    strict = col < row

    def block(j, carries, diag):
        start = pl.multiple_of(j * TK, TK)
        new_carries, pvs = [], []
        for g in range(N_GROUPS):
            cols = slice(g * LANES, (g + 1) * LANES)
            kb = k_ref[pl.ds(start, TK), cols]
            v_heads = _lane_halves(v_ref[pl.ds(start, TK), cols])
            ws = []
            for sub in range(2):
                h = 2 * g + sub
                z = _nt_dot(q_heads[h], kb)
                sp = _softplus2(z)
                if diag:
                    sp = jnp.where(strict, sp, 0.0)
                hi = sp.astype(BF16)
                lo = (sp - hi.astype(F32)).astype(BF16)
                suffix = _dot(jnp.concatenate([hi, lo], axis=1), tri2)
                w = jnp.exp2(z - suffix - carries[h])
                if diag:
                    w = jnp.where(strict, w, 0.0)
                new_carries.append(carries[h] + jnp.sum(sp, axis=1, keepdims=True))
                ws.append(w.astype(BF16))
            pvs.append(_dot(jnp.concatenate(ws, axis=1), jnp.concatenate(v_heads, axis=0)))
        return tuple(new_carries), pvs

    def skip_slack(carries):
        least = _lane_pack([jnp.min(c, axis=0, keepdims=True) for c in carries])
        return _lane_reduce(least - zbound, SB_HEADS, jnp.min, jnp.inf)

    zeros = (jnp.zeros((TQ, 1), F32),) * SB_HEADS

    def diagonal_and_previous(_):
        carries, pv_diag = block(i, zeros, True)
        carries, pv_prev = block(i - 1, carries, False)
        for g in range(N_GROUPS):
            o_ref[:, g * LANES:(g + 1) * LANES] = pv_diag[g] + pv_prev[g]
        return (jnp.int32(1), skip_slack(carries)) + carries

    def diagonal_only(_):
        carries, pv_diag = block(i, zeros, True)
        for g in range(N_GROUPS):
            o_ref[:, g * LANES:(g + 1) * LANES] = pv_diag[g]
        return (jnp.int32(0), skip_slack(carries)) + carries

    state = lax.cond(i > 0, diagonal_and_previous, diagonal_only, 0)

    def cond(state):
        return jnp.logical_and(state[0] < i, state[1] < SKIP_LOG2)

    def body(state):
        t = state[0]
        carries, pvs = block(i - 1 - t, state[2:], False)
        for g in range(N_GROUPS):
            o_ref[:, g * LANES:(g + 1) * LANES] += pvs[g]
        return (t + 1, skip_slack(carries)) + carries

    lax.while_loop(cond, body, state)


def _sb_attention(proj, tri2, gsum, batch, seq):
    nq = seq // TQ
    t = proj.shape[0]
    return pl.pallas_call(
        _sb_kernel,
        grid=(batch, nq),
        in_specs=[
            pl.BlockSpec((seq, SLAB), lambda b, i: (b, 0)),
            pl.BlockSpec((seq, SLAB), lambda b, i: (b, 1)),
            pl.BlockSpec((seq, SLAB), lambda b, i: (b, 2)),
            pl.BlockSpec((2 * TK, TK), lambda b, i: (0, 0)),
            pl.BlockSpec((SLAB, LANES), lambda b, i: (0, 0)),
        ],
        out_specs=pl.BlockSpec((TQ, SLAB), lambda b, i: (b * nq + i, 0)),
        out_shape=jax.ShapeDtypeStruct((t, SLAB), F32),
        scratch_shapes=[pltpu.VMEM((1, LANES), F32)],
        compiler_params=pltpu.CompilerParams(
            dimension_semantics=("arbitrary", "arbitrary"), vmem_limit_bytes=VMEM_LIMIT),
        name="sb_attention",
    )(proj, proj, proj, tri2, gsum)


def _df_kernel(slope_ref, q_ref, k_ref, v_ref, bias_ref, dbias_ref, lq_ref, lk_ref,
               gsum_ref, o_ref, acc_ref, plan_ref, *, lambda_init):
    i = pl.program_id(1)
    n_chain = 2 * DF_HEADS

    @pl.when(i == 0)
    def _():
        lbound = _logit_bounds(q_ref, k_ref, gsum_ref)
        worst = 2.0 * _lane_reduce(lbound, n_chain, jnp.max, -jnp.inf)
        plan_ref[0] = (worst <= MAX_FIXED_REF_LOG2).astype(jnp.int32)
        reach = (2.0 * _lane_reduce(lbound, 2, jnp.max, -jnp.inf) + SKIP_LOG2) / slope_ref[0]
        kept = jnp.int32(1)
        for t in range(1, k_ref.shape[0] // TK):
            kept += (reach >= t).astype(jnp.int32)
        plan_ref[1] = kept

    fixed_ref_ok = plan_ref[0] == 1
    n_near = jnp.minimum(plan_ref[1], i)

    q_rows = pl.ds(pl.multiple_of(i * TQ, TQ), TQ)
    q_comps = []
    for hd in range(DF_HEADS):
        q_comps += list(_lane_halves(q_ref[q_rows, hd * LANES:(hd + 1) * LANES]))

    def scores(j, c, bias):
        hd = c // 2
        start = pl.multiple_of(j * TK, TK)
        return _nt_dot(q_comps[c], k_ref[pl.ds(start, TK), hd * LANES:(hd + 1) * LANES]) + bias

    ones = jnp.ones((TK, LANES), BF16)

    def values(j, hd):
        vb = v_ref[pl.ds(pl.multiple_of(j * TK, TK), TK), hd * LANES:(hd + 1) * LANES]
        return jnp.concatenate([vb, ones], axis=1)

    def block_shift(t, hd):
        return slope_ref[hd] * jnp.asarray(t + 1).astype(F32)

    def diagonal(c):
        s = scores(i, c, dbias_ref[c // 2])
        m = jnp.max(s, axis=1, keepdims=True)
        return m, jnp.exp2(s - m).astype(BF16)

    def earlier(t, c, m):
        hd = c // 2
        return jnp.exp2(scores(i - 1 - t, c, bias_ref[hd]) - (m + block_shift(t, hd))).astype(BF16)

    def contract(ps, js, hd):
        return _dot(jnp.concatenate(ps, axis=1), jnp.concatenate([values(j, hd) for j in js], axis=0))

    @pl.when(jnp.logical_and(fixed_ref_ok, i == 0))
    def _():
        for c in range(n_chain):
            acc_ref[c] = contract([diagonal(c)[1]], [i], c // 2)

    @pl.when(jnp.logical_and(fixed_ref_ok, i > 0))
    def _():
        ms = []
        for c in range(n_chain):
            m, p = diagonal(c)
            ms.append(m)
            acc_ref[c] = contract([p, earlier(0, c, m)], [i, i - 1], c // 2)

        def add_blocks(ts, chains):
            for c in chains:
                ps = [earlier(t, c, ms[c]) for t in ts]
                acc_ref[c] += contract(ps, [i - 1 - t for t in ts], c // 2)

        def near(t, carry):
            add_blocks((t,), range(n_chain))
            return carry

        lax.fori_loop(1, n_near, near, 0)

        far_chains = range(2, n_chain)
        n_far = i - n_near

        def far_pair(u, carry):
            add_blocks((n_near + 2 * u, n_near + 2 * u + 1), far_chains)
            return carry

        lax.fori_loop(0, lax.shift_right_logical(n_far, 1), far_pair, 0)

        @pl.when(lax.bitwise_and(n_far, 1) == 1)
        def _():
            add_blocks((i - 1,), far_chains)

    @pl.when(jnp.logical_not(fixed_ref_ok))
    def _():
        ms = []
        for c in range(n_chain):
            m, p = diagonal(c)
            ms.append(m)
            acc_ref[c] = contract([p], [i], c // 2)

        def body(t, m_run):
            j = i - 1 - t
            m_out = []
            for c in range(n_chain):
                hd = c // 2
                shift = block_shift(t, hd)
                s = scores(j, c, bias_ref[hd])
                m_new = jnp.maximum(m_run[c], jnp.max(s, axis=1, keepdims=True) - shift)
                alpha = jnp.exp2(m_run[c] - m_new)
                p = jnp.exp2(s - (m_new + shift))
                m_out.append(m_new)
                acc_ref[c] = alpha * acc_ref[c] + _dot(p.astype(BF16), values(j, hd))
            return tuple(m_out)

        lax.fori_loop(0, i, body, tuple(ms))

    dots = jnp.sum(lq_ref[...] * lk_ref[...], axis=1, keepdims=True)
    lam = jnp.exp(dots[0:1]) - jnp.exp(dots[1:2]) + lambda_init
    for hd in range(DF_HEADS):
        a0, a1 = acc_ref[2 * hd], acc_ref[2 * hd + 1]
        o_ref[:, hd * LANES:(hd + 1) * LANES] = (
            a0[:, :LANES] / a0[:, LANES:] - lam * (a1[:, :LANES] / a1[:, LANES:]))


def _df_attention(proj, slopes, bias, dbias, lq, lk, gsum, batch, seq, lambda_init):
    nq = seq // TQ
    t = proj.shape[0]
    const3 = lambda b, i: (0, 0, 0)
    const2 = lambda b, i: (0, 0)
    return pl.pallas_call(
        functools.partial(_df_kernel, lambda_init=lambda_init),
        grid=(batch, nq),
        in_specs=[
            pl.BlockSpec(memory_space=pltpu.SMEM),
            pl.BlockSpec((seq, SLAB), lambda b, i: (b, 4)),
            pl.BlockSpec((seq, SLAB), lambda b, i: (b, 5)),
            pl.BlockSpec((seq, SLAB), lambda b, i: (b, 6)),
            pl.BlockSpec((DF_HEADS, TQ, TK), const3),
            pl.BlockSpec((DF_HEADS, TQ, TK), const3),
            pl.BlockSpec((2, HEAD_DIM), const2),
            pl.BlockSpec((2, HEAD_DIM), const2),
            pl.BlockSpec((SLAB, LANES), const2),
        ],
        out_specs=pl.BlockSpec((TQ, SLAB), lambda b, i: (b * nq + i, 0)),
        out_shape=jax.ShapeDtypeStruct((t, SLAB), F32),
        scratch_shapes=[pltpu.VMEM((2 * DF_HEADS, TQ, 2 * LANES), F32),
                        pltpu.SMEM((2,), jnp.int32)],
        compiler_params=pltpu.CompilerParams(
            dimension_semantics=("arbitrary", "arbitrary"), vmem_limit_bytes=VMEM_LIMIT),
        name="df_attention",
    )(slopes, proj, proj, proj, bias, dbias, lq, lk, gsum)


def _out_kernel(x_ref, osb_ref, odf_ref, zsb_ref, zdf_ref, gpre_ref, gpost_ref, ghead_ref,
                wg_ref, bg_ref, wsb_ref, wdf_ref, wo_ref, o_ref, m_ref,
                *, col_chunk, row_block, lambda_init):
    d = x_ref.shape[1]
    for r in range(0, x_ref.shape[0], row_block):
        rows = slice(r, r + row_block)
        x = x_ref[rows, :]
        h = _rms(x, gpre_ref[...]).astype(BF16)
        a_sb = (osb_ref[rows, :] * _silu(zsb_ref[rows, :].astype(F32))).astype(BF16)
        a_df = []
        for hd in range(DF_HEADS):
            cols = slice(hd * LANES, (hd + 1) * LANES)
            o = _rms(odf_ref[rows, cols], ghead_ref[...]) * (1.0 - lambda_init)
            a_df.append((o * _silu(zdf_ref[rows, cols].astype(F32))).astype(BF16))
        a_df = jnp.concatenate(a_df, axis=1)
        for n in range(0, d, col_chunk):
            cols = slice(n, n + col_chunk)
            dcols = slice(d + n, d + n + col_chunk)
            g_sb = jax.nn.sigmoid(_dot(h, wg_ref[:, cols]) + bg_ref[:, cols])
            g_df = jax.nn.sigmoid(_dot(h, wg_ref[:, dcols]) + bg_ref[:, dcols])
            y_sb = _dot(a_sb, wsb_ref[:, cols])
            y_df = _dot(a_df, wdf_ref[:, cols])
            m_ref[rows, cols] = (g_sb * y_sb + g_df * y_df).astype(BF16)
        out = _dot(m_ref[rows, :], wo_ref[...])
        o_ref[rows, :] = x + _rms(out, gpost_ref[...])


def _output_stage(xf, o_sb, o_df, proj, gpre, gpost, ghead, wg, bg, wsb, wdf, wo, lambda_init):
    t, d = xf.shape
    const = lambda i: (0, 0)
    resident = lambda a: pl.BlockSpec(a.shape, const, pipeline_mode=pl.Buffered(1))
    rows = lambda width, col: pl.BlockSpec((TM_OUT, width), lambda i: (i, col))
    return pl.pallas_call(
        functools.partial(_out_kernel, col_chunk=512, row_block=256, lambda_init=lambda_init),
        grid=(t // TM_OUT,),
        in_specs=[
            rows(d, 0),
            rows(SLAB, 0),
            rows(SLAB, 0),
            rows(SLAB, 3),
            rows(SLAB, 7),
            pl.BlockSpec((1, d), const),
            pl.BlockSpec((1, d), const),
            pl.BlockSpec((1, LANES), const),
            resident(wg), pl.BlockSpec(bg.shape, const), resident(wsb), resident(wdf), resident(wo),
        ],
        out_specs=rows(d, 0),
        out_shape=jax.ShapeDtypeStruct((t, d), F32),
        scratch_shapes=[pltpu.VMEM((TM_OUT, d), BF16)],
        compiler_params=pltpu.CompilerParams(
            dimension_semantics=("arbitrary",), vmem_limit_bytes=VMEM_LIMIT),
        name="output_stage",
    )(xf, o_sb, o_df, proj, proj, gpre, gpost, ghead, wg, bg, wsb, wdf, wo)


def _constant_tiles():
    r = np.arange(TQ)[:, None]
    c = np.arange(TK)[None, :]
    tri = (r >= c).astype(np.float32)
    tri2 = np.concatenate([tri, tri], axis=0)
    gsum = (np.arange(SLAB)[:, None] // HEAD_DIM == np.arange(LANES)[None, :]).astype(np.float32)
    slopes = np.array([2.0 ** (-8.0 * (h + 1) / DF_HEADS) for h in range(DF_HEADS)],
                      dtype=np.float64) * LOG2E
    rel = (r - c).astype(np.float64)
    bias = -slopes[:, None, None] * rel[None]
    visible = (c // CHUNK) <= (r // CHUNK)
    dbias = np.where(visible[None], -slopes[:, None, None] * np.abs(rel)[None], NEG_BIG)
    block_slopes = slopes * TQ
    return (jnp.asarray(tri2, BF16), jnp.asarray(gsum, BF16), jnp.asarray(block_slopes, F32),
            jnp.asarray(bias, F32), jnp.asarray(dbias, F32))


def kernel(x, norm_pre, norm_post, w_in, w_gate, b_gate, w_branch_sb, w_branch_df, w_out,
           lambda_q, lambda_k, df_head_norm):
    b, s, d = x.shape
    depth = w_in.shape[0]
    assert s % TQ == 0 and TQ == TK and TQ % CHUNK == 0
    assert (b * s) % TM_PROJ == 0 and (b * s) % TM_OUT == 0
    tri2, gsum, block_slopes, bias, dbias = _constant_tiles()
    xf = x.reshape(b * s, d)
    for l in range(depth):
        lambda_init = 0.8 - 0.6 * float(np.exp(-0.3 * l))
        proj = _in_projection(xf, norm_pre[l][None], w_in[l].astype(BF16))
        o_sb = _sb_attention(proj, tri2, gsum, b, s)
        o_df = _df_attention(proj, block_slopes, bias, dbias, lambda_q[l].astype(F32),
                             lambda_k[l].astype(F32), gsum, b, s, lambda_init)
        xf = _output_stage(xf, o_sb, o_df, proj, norm_pre[l][None], norm_post[l][None],
                           df_head_norm[l][None].astype(F32),
                           w_gate[l].astype(BF16), b_gate[l][None],
                           w_branch_sb[l].astype(BF16), w_branch_df[l].astype(BF16),
                           w_out[l].astype(BF16), lambda_init)
    return xf.reshape(b, s, d)
```

```python
import functools
import math

import numpy as np
import jax
import jax.numpy as jnp
from jax import lax
from jax.experimental import pallas as pl
from jax.experimental.pallas import tpu as pltpu

F32 = jnp.float32
BF16 = jnp.bfloat16

EPS = 1e-6
CHUNK = 64
HEAD_DIM = 64
LANES = 128
N_GROUPS = 4
SLAB = N_GROUPS * LANES
SB_HEADS = SLAB // HEAD_DIM
DF_HEADS = 4
LOG2E = math.log2(math.e)
Q_SCALE = HEAD_DIM ** -0.5 * LOG2E

TQ = 256
TK = 256
TM_PROJ = 1024
TM_OUT = 1024
NEG_BIG = -1e30
SKIP_LOG2 = 135.0
BOUND_MARGIN = 1.01
MAX_FIXED_REF_LOG2 = 100.0
SOFTPLUS_CLAMP_LOG2 = 100.0

VMEM_LIMIT = 56 * 1024 * 1024


def _nt_dot(a, b):
    return lax.dot_general(a, b, (((1,), (1,)), ((), ())), preferred_element_type=F32)


def _dot(a, b):
    return jnp.dot(a, b, preferred_element_type=F32)


def _rms(x, gain):
    return x * lax.rsqrt(jnp.mean(x * x, axis=-1, keepdims=True) + EPS) * gain


def _silu(z):
    return z / (1.0 + jnp.exp(-z))


def _lane_halves(x):
    lane = lax.broadcasted_iota(jnp.int32, (1, LANES), 1)
    first = lane < HEAD_DIM
    zero = jnp.zeros_like(x)
    return jnp.where(first, x, zero), jnp.where(first, zero, x)


def _max_sq_group_norms(ref, gsum_ref):
    best = jnp.zeros((1, LANES), F32)
    for r in range(0, ref.shape[0], TK):
        xf = ref[r:r + TK, :].astype(F32)
        n2 = _dot((xf * xf).astype(BF16), gsum_ref[...])
        best = jnp.maximum(best, jnp.max(n2, axis=0, keepdims=True))
    return best


def _logit_bounds(qall_ref, k_ref, gsum_ref):
    qn = _max_sq_group_norms(qall_ref, gsum_ref)
    kn = _max_sq_group_norms(k_ref, gsum_ref)
    return jnp.sqrt(qn * kn) * BOUND_MARGIN


def _lane_pack(column_values):
    lane = lax.broadcasted_iota(jnp.int32, (1, LANES), 1)
    row = jnp.zeros((1, LANES), F32)
    for h, v in enumerate(column_values):
        row = jnp.where(lane == h, v, row)
    return row


def _lane_reduce(row, n, op, fill):
    lane = lax.broadcasted_iota(jnp.int32, (1, LANES), 1)
    return op(jnp.where(lane < n, row, fill))


def _proj_kernel(x_ref, g_ref, w_ref, o_ref, *, col_chunk, q_chunks, row_block):
    n_cols = o_ref.shape[1]
    for r in range(0, x_ref.shape[0], row_block):
        rows = slice(r, r + row_block)
        h = _rms(x_ref[rows, :], g_ref[...]).astype(BF16)
        for n in range(0, n_cols, col_chunk):
            y = _dot(h, w_ref[:, n:n + col_chunk])
            if n // col_chunk in q_chunks:
                y = y * Q_SCALE
            o_ref[rows, n:n + col_chunk] = y.astype(BF16)


def _in_projection(xf, gain, w):
    t, d = xf.shape
    n = w.shape[1]
    return pl.pallas_call(
        functools.partial(_proj_kernel, col_chunk=SLAB, q_chunks=(0, 4), row_block=512),
        grid=(t // TM_PROJ,),
        in_specs=[
            pl.BlockSpec((TM_PROJ, d), lambda i: (i, 0)),
            pl.BlockSpec((1, d), lambda i: (0, 0)),
            pl.BlockSpec((d, n), lambda i: (0, 0), pipeline_mode=pl.Buffered(1)),
        ],
        out_specs=pl.BlockSpec((TM_PROJ, n), lambda i: (i, 0)),
        out_shape=jax.ShapeDtypeStruct((t, n), BF16),
        compiler_params=pltpu.CompilerParams(
            dimension_semantics=("arbitrary",), vmem_limit_bytes=VMEM_LIMIT),
        name="in_projection",
    )(xf, gain, w)


def _softplus2(z2):
    return jnp.maximum(z2, jnp.log2(1.0 + jnp.exp2(jnp.minimum(z2, SOFTPLUS_CLAMP_LOG2))))


def _sb_kernel(q_ref, k_ref, v_ref, tri2_ref, gsum_ref, o_ref, zb_ref):
    i = pl.program_id(1)

    @pl.when(i == 0)
    def _():
        zb_ref[...] = _logit_bounds(q_ref, k_ref, gsum_ref)

    zbound = zb_ref[...]

    q_rows = pl.ds(pl.multiple_of(i * TQ, TQ), TQ)
    q_heads = []
    for g in range(N_GROUPS):
        q_heads += list(_lane_halves(q_ref[q_rows, g * LANES:(g + 1) * LANES]))
    tri2 = tri2_ref[...]

    row = lax.broadcasted_iota(jnp.int32, (TQ, TK), 0)
    col = lax.broadcasted_iota(jnp.int32, (TQ, TK), 1)
    strict = col < row

    def block(j, carries, diag):
        start = pl.multiple_of(j * TK, TK)
        new_carries, pvs = [], []
        for g in range(N_GROUPS):
            cols = slice(g * LANES, (g + 1) * LANES)
            kb = k_ref[pl.ds(start, TK), cols]
            v_heads = _lane_halves(v_ref[pl.ds(start, TK), cols])
            ws = []
            for sub in range(2):
                h = 2 * g + sub
                z = _nt_dot(q_heads[h], kb)
                sp = _softplus2(z)
                if diag:
                    sp = jnp.where(strict, sp, 0.0)
                hi = sp.astype(BF16)
                lo = (sp - hi.astype(F32)).astype(BF16)
                suffix = _dot(jnp.concatenate([hi, lo], axis=1), tri2)
                w = jnp.exp2(z - suffix - carries[h])
                if diag:
                    w = jnp.where(strict, w, 0.0)
                new_carries.append(carries[h] + jnp.sum(sp, axis=1, keepdims=True))
                ws.append(w.astype(BF16))
            pvs.append(_dot(jnp.concatenate(ws, axis=1), jnp.concatenate(v_heads, axis=0)))
        return tuple(new_carries), pvs

    def skip_slack(carries):
        least = _lane_pack([jnp.min(c, axis=0, keepdims=True) for c in carries])
        return _lane_reduce(least - zbound, SB_HEADS, jnp.min, jnp.inf)

    zeros = (jnp.zeros((TQ, 1), F32),) * SB_HEADS

    def diagonal_and_previous(_):
        carries, pv_diag = block(i, zeros, True)
        carries, pv_prev = block(i - 1, carries, False)
        for g in range(N_GROUPS):
            o_ref[:, g * LANES:(g + 1) * LANES] = pv_diag[g] + pv_prev[g]
        return (jnp.int32(1), skip_slack(carries)) + carries

    def diagonal_only(_):
        carries, pv_diag = block(i, zeros, True)
        for g in range(N_GROUPS):
            o_ref[:, g * LANES:(g + 1) * LANES] = pv_diag[g]
        return (jnp.int32(0), skip_slack(carries)) + carries

    state = lax.cond(i > 0, diagonal_and_previous, diagonal_only, 0)

    def cond(state):
        return jnp.logical_and(state[0] < i, state[1] < SKIP_LOG2)

    def body(state):
        t = state[0]
        carries, pvs = block(i - 1 - t, state[2:], False)
        for g in range(N_GROUPS):
            o_ref[:, g * LANES:(g + 1) * LANES] += pvs[g]
        return (t + 1, skip_slack(carries)) + carries

    lax.while_loop(cond, body, state)


def _sb_attention(proj, tri2, gsum, batch, seq):
    nq = seq // TQ
    t = proj.shape[0]
    return pl.pallas_call(
        _sb_kernel,
        grid=(batch, nq),
        in_specs=[
            pl.BlockSpec((seq, SLAB), lambda b, i: (b, 0)),
            pl.BlockSpec((seq, SLAB), lambda b, i: (b, 1)),
            pl.BlockSpec((seq, SLAB), lambda b, i: (b, 2)),
            pl.BlockSpec((2 * TK, TK), lambda b, i: (0, 0)),
            pl.BlockSpec((SLAB, LANES), lambda b, i: (0, 0)),
        ],
        out_specs=pl.BlockSpec((TQ, SLAB), lambda b, i: (b * nq + i, 0)),
        out_shape=jax.ShapeDtypeStruct((t, SLAB), F32),
        scratch_shapes=[pltpu.VMEM((1, LANES), F32)],
        compiler_params=pltpu.CompilerParams(
            dimension_semantics=("arbitrary", "arbitrary"), vmem_limit_bytes=VMEM_LIMIT),
        name="sb_attention",
    )(proj, proj, proj, tri2, gsum)


def _df_kernel(slope_ref, q_ref, k_ref, v_ref, bias_ref, dbias_ref, lq_ref, lk_ref,
               gsum_ref, o_ref, acc_ref, plan_ref, *, lambda_init):
    i = pl.program_id(1)
    n_chain = 2 * DF_HEADS

    @pl.when(i == 0)
    def _():
        lbound = _logit_bounds(q_ref, k_ref, gsum_ref)
        worst = 2.0 * _lane_reduce(lbound, n_chain, jnp.max, -jnp.inf)
        plan_ref[0] = (worst <= MAX_FIXED_REF_LOG2).astype(jnp.int32)
        reach = (2.0 * _lane_reduce(lbound, 2, jnp.max, -jnp.inf) + SKIP_LOG2) / slope_ref[0]
        kept = jnp.int32(1)
        for t in range(1, k_ref.shape[0] // TK):
            kept += (reach >= t).astype(jnp.int32)
        plan_ref[1] = kept

    fixed_ref_ok = plan_ref[0] == 1
    n_near = jnp.minimum(plan_ref[1], i)

    q_rows = pl.ds(pl.multiple_of(i * TQ, TQ), TQ)
    q_comps = []
    for hd in range(DF_HEADS):
        q_comps += list(_lane_halves(q_ref[q_rows, hd * LANES:(hd + 1) * LANES]))

    def scores(j, c, bias):
        hd = c // 2
        start = pl.multiple_of(j * TK, TK)
        return _nt_dot(q_comps[c], k_ref[pl.ds(start, TK), hd * LANES:(hd + 1) * LANES]) + bias

    ones = jnp.ones((TK, LANES), BF16)

    def values(j, hd):
        vb = v_ref[pl.ds(pl.multiple_of(j * TK, TK), TK), hd * LANES:(hd + 1) * LANES]
        return jnp.concatenate([vb, ones], axis=1)

    def block_shift(t, hd):
        return slope_ref[hd] * jnp.asarray(t + 1).astype(F32)

    def diagonal(c):
        s = scores(i, c, dbias_ref[c // 2])
        m = jnp.max(s, axis=1, keepdims=True)
        return m, jnp.exp2(s - m).astype(BF16)

    def earlier(t, c, m):
        hd = c // 2
        return jnp.exp2(scores(i - 1 - t, c, bias_ref[hd]) - (m + block_shift(t, hd))).astype(BF16)

    def contract(ps, js, hd):
        return _dot(jnp.concatenate(ps, axis=1), jnp.concatenate([values(j, hd) for j in js], axis=0))

    def start_with(n_merged):
        ms = []
        for c in range(n_chain):
            m, p = diagonal(c)
            ms.append(m)
            ps = [p] + [earlier(t, c, m) for t in range(n_merged)]
            acc_ref[c] = contract(ps, [i - t for t in range(n_merged + 1)], c // 2)
        return ms

    @pl.when(jnp.logical_and(fixed_ref_ok, i == 0))
    def _():
        start_with(0)

    @pl.when(jnp.logical_and(fixed_ref_ok, i == 1))
    def _():
        start_with(1)

    @pl.when(jnp.logical_and(fixed_ref_ok, i >= 2))
    def _():
        ms = start_with(2)

        def add_blocks(ts, chains):
            for c in chains:
                ps = [earlier(t, c, ms[c]) for t in ts]
                acc_ref[c] += contract(ps, [i - 1 - t for t in ts], c // 2)

        first_far = jnp.maximum(n_near, 2)

        def near(t, carry):
            add_blocks((t,), range(n_chain))
            return carry

        lax.fori_loop(2, first_far, near, 0)

        far_chains = range(2, n_chain)
        n_far = i - first_far

        def far_pair(u, carry):
            add_blocks((first_far + 2 * u, first_far + 2 * u + 1), far_chains)
            return carry

        lax.fori_loop(0, lax.shift_right_logical(n_far, 1), far_pair, 0)

        @pl.when(lax.bitwise_and(n_far, 1) == 1)
        def _():
            add_blocks((i - 1,), far_chains)

    @pl.when(jnp.logical_not(fixed_ref_ok))
    def _():
        ms = []
        for c in range(n_chain):
            m, p = diagonal(c)
            ms.append(m)
            acc_ref[c] = contract([p], [i], c // 2)

        def body(t, m_run):
            j = i - 1 - t
            m_out = []
            for c in range(n_chain):
                hd = c // 2
                shift = block_shift(t, hd)
                s = scores(j, c, bias_ref[hd])
                m_new = jnp.maximum(m_run[c], jnp.max(s, axis=1, keepdims=True) - shift)
                alpha = jnp.exp2(m_run[c] - m_new)
                p = jnp.exp2(s - (m_new + shift))
                m_out.append(m_new)
                acc_ref[c] = alpha * acc_ref[c] + _dot(p.astype(BF16), values(j, hd))
            return tuple(m_out)

        lax.fori_loop(0, i, body, tuple(ms))

    dots = jnp.sum(lq_ref[...] * lk_ref[...], axis=1, keepdims=True)
    lam = jnp.exp(dots[0:1]) - jnp.exp(dots[1:2]) + lambda_init
    for hd in range(DF_HEADS):
        a0, a1 = acc_ref[2 * hd], acc_ref[2 * hd + 1]
        o_ref[:, hd * LANES:(hd + 1) * LANES] = (
            a0[:, :LANES] / a0[:, LANES:] - lam * (a1[:, :LANES] / a1[:, LANES:]))


def _df_attention(proj, slopes, bias, dbias, lq, lk, gsum, batch, seq, lambda_init):
    nq = seq // TQ
    t = proj.shape[0]
    const3 = lambda b, i: (0, 0, 0)
    const2 = lambda b, i: (0, 0)
    return pl.pallas_call(
        functools.partial(_df_kernel, lambda_init=lambda_init),
        grid=(batch, nq),
        in_specs=[
            pl.BlockSpec(memory_space=pltpu.SMEM),
            pl.BlockSpec((seq, SLAB), lambda b, i: (b, 4)),
            pl.BlockSpec((seq, SLAB), lambda b, i: (b, 5)),
            pl.BlockSpec((seq, SLAB), lambda b, i: (b, 6)),
            pl.BlockSpec((DF_HEADS, TQ, TK), const3),
            pl.BlockSpec((DF_HEADS, TQ, TK), const3),
            pl.BlockSpec((2, HEAD_DIM), const2),
            pl.BlockSpec((2, HEAD_DIM), const2),
            pl.BlockSpec((SLAB, LANES), const2),
        ],
        out_specs=pl.BlockSpec((TQ, SLAB), lambda b, i: (b * nq + i, 0)),
        out_shape=jax.ShapeDtypeStruct((t, SLAB), F32),
        scratch_shapes=[pltpu.VMEM((2 * DF_HEADS, TQ, 2 * LANES), F32),
                        pltpu.SMEM((2,), jnp.int32)],
        compiler_params=pltpu.CompilerParams(
            dimension_semantics=("arbitrary", "arbitrary"), vmem_limit_bytes=VMEM_LIMIT),
        name="df_attention",
    )(slopes, proj, proj, proj, bias, dbias, lq, lk, gsum)


def _out_kernel(x_ref, osb_ref, odf_ref, zsb_ref, zdf_ref, gpre_ref, gpost_ref, ghead_ref,
                wg_ref, bg_ref, wsb_ref, wdf_ref, wo_ref, o_ref, m_ref,
                *, col_chunk, row_block, lambda_init):
    d = x_ref.shape[1]
    for r in range(0, x_ref.shape[0], row_block):
        rows = slice(r, r + row_block)
        x = x_ref[rows, :]
        h = _rms(x, gpre_ref[...]).astype(BF16)
        a_sb = (osb_ref[rows, :] * _silu(zsb_ref[rows, :].astype(F32))).astype(BF16)
        a_df = []
        for hd in range(DF_HEADS):
            cols = slice(hd * LANES, (hd + 1) * LANES)
            o = _rms(odf_ref[rows, cols], ghead_ref[...]) * (1.0 - lambda_init)
            a_df.append((o * _silu(zdf_ref[rows, cols].astype(F32))).astype(BF16))
        a_df = jnp.concatenate(a_df, axis=1)
        for n in range(0, d, col_chunk):
            cols = slice(n, n + col_chunk)
            dcols = slice(d + n, d + n + col_chunk)
            g_sb = jax.nn.sigmoid(_dot(h, wg_ref[:, cols]) + bg_ref[:, cols])
            g_df = jax.nn.sigmoid(_dot(h, wg_ref[:, dcols]) + bg_ref[:, dcols])
            y_sb = _dot(a_sb, wsb_ref[:, cols])
            y_df = _dot(a_df, wdf_ref[:, cols])
            m_ref[rows, cols] = (g_sb * y_sb + g_df * y_df).astype(BF16)
        out = _dot(m_ref[rows, :], wo_ref[...])
        o_ref[rows, :] = x + _rms(out, gpost_ref[...])


def _output_stage(xf, o_sb, o_df, proj, gpre, gpost, ghead, wg, bg, wsb, wdf, wo, lambda_init):
    t, d = xf.shape
    const = lambda i: (0, 0)
    resident = lambda a: pl.BlockSpec(a.shape, const, pipeline_mode=pl.Buffered(1))
    rows = lambda width, col: pl.BlockSpec((TM_OUT, width), lambda i: (i, col))
    return pl.pallas_call(
        functools.partial(_out_kernel, col_chunk=512, row_block=256, lambda_init=lambda_init),
        grid=(t // TM_OUT,),
        in_specs=[
            rows(d, 0),
            rows(SLAB, 0),
            rows(SLAB, 0),
            rows(SLAB, 3),
            rows(SLAB, 7),
            pl.BlockSpec((1, d), const),
            pl.BlockSpec((1, d), const),
            pl.BlockSpec((1, LANES), const),
            resident(wg), pl.BlockSpec(bg.shape, const), resident(wsb), resident(wdf), resident(wo),
        ],
        out_specs=rows(d, 0),
        out_shape=jax.ShapeDtypeStruct((t, d), F32),
        scratch_shapes=[pltpu.VMEM((TM_OUT, d), BF16)],
        compiler_params=pltpu.CompilerParams(
            dimension_semantics=("arbitrary",), vmem_limit_bytes=VMEM_LIMIT),
        name="output_stage",
    )(xf, o_sb, o_df, proj, proj, gpre, gpost, ghead, wg, bg, wsb, wdf, wo)


def _constant_tiles():
    r = np.arange(TQ)[:, None]
    c = np.arange(TK)[None, :]
    tri = (r >= c).astype(np.float32)
    tri2 = np.concatenate([tri, tri], axis=0)
    gsum = (np.arange(SLAB)[:, None] // HEAD_DIM == np.arange(LANES)[None, :]).astype(np.float32)
    slopes = np.array([2.0 ** (-8.0 * (h + 1) / DF_HEADS) for h in range(DF_HEADS)],
                      dtype=np.float64) * LOG2E
    rel = (r - c).astype(np.float64)
    bias = -slopes[:, None, None] * rel[None]
    visible = (c // CHUNK) <= (r // CHUNK)
    dbias = np.where(visible[None], -slopes[:, None, None] * np.abs(rel)[None], NEG_BIG)
    block_slopes = slopes * TQ
    return (jnp.asarray(tri2, BF16), jnp.asarray(gsum, BF16), jnp.asarray(block_slopes, F32),
            jnp.asarray(bias, F32), jnp.asarray(dbias, F32))


def kernel(x, norm_pre, norm_post, w_in, w_gate, b_gate, w_branch_sb, w_branch_df, w_out,
           lambda_q, lambda_k, df_head_norm):
    b, s, d = x.shape
    depth = w_in.shape[0]
    assert s % TQ == 0 and TQ == TK and TQ % CHUNK == 0
    assert (b * s) % TM_PROJ == 0 and (b * s) % TM_OUT == 0
    tri2, gsum, block_slopes, bias, dbias = _constant_tiles()
    xf = x.reshape(b * s, d)
    for l in range(depth):
        lambda_init = 0.8 - 0.6 * float(np.exp(-0.3 * l))
        proj = _in_projection(xf, norm_pre[l][None], w_in[l].astype(BF16))
        o_sb = _sb_attention(proj, tri2, gsum, b, s)
        o_df = _df_attention(proj, block_slopes, bias, dbias, lambda_q[l].astype(F32),
                             lambda_k[l].astype(F32), gsum, b, s, lambda_init)
        xf = _output_stage(xf, o_sb, o_df, proj, norm_pre[l][None], norm_post[l][None],
                           df_head_norm[l][None].astype(F32),
                           w_gate[l].astype(BF16), b_gate[l][None],
                           w_branch_sb[l].astype(BF16), w_branch_df[l].astype(BF16),
                           w_out[l].astype(BF16), lambda_init)
    return xf.reshape(b, s, d)
```

```python
import functools
import math

import numpy as np
import jax
import jax.numpy as jnp
from jax import lax
from jax.experimental import pallas as pl
from jax.experimental.pallas import tpu as pltpu

F32 = jnp.float32
BF16 = jnp.bfloat16

EPS = 1e-6
CHUNK = 64
HEAD_DIM = 64
LANES = 128
N_GROUPS = 4
SLAB = N_GROUPS * LANES
SB_HEADS = SLAB // HEAD_DIM
DF_HEADS = 4
LOG2E = math.log2(math.e)
Q_SCALE = HEAD_DIM ** -0.5 * LOG2E
Q_SLABS = (0, 4)
SB_Q_HALVES = (0, 8)
DF_Q_HALVES = (4, 9)

TQ = 256
TK = 256
SB_TILE = 256
SB_MERGED_PREVIOUS = 1
TM_PROJ = 1024
TM_OUT = 1024
NEG_BIG = -1e30
SKIP_LOG2 = 135.0
BOUND_MARGIN = 1.01
MAX_FIXED_REF_LOG2 = 100.0
SOFTPLUS_CLAMP_LOG2 = 100.0

VMEM_LIMIT = 56 * 1024 * 1024


def _nt_dot(a, b):
    return lax.dot_general(a, b, (((1,), (1,)), ((), ())), preferred_element_type=F32)


def _dot(a, b):
    return jnp.dot(a, b, preferred_element_type=F32)


def _rms(x, gain):
    return x * lax.rsqrt(jnp.mean(x * x, axis=-1, keepdims=True) + EPS) * gain


def _silu(z):
    return z * jax.nn.sigmoid(z)


def _lane_halves(x):
    lane = lax.broadcasted_iota(jnp.int32, (1, LANES), 1)
    first = lane < HEAD_DIM
    zero = jnp.zeros_like(x)
    return jnp.where(first, x, zero), jnp.where(first, zero, x)


def _max_sq_group_norms(refs, gsum_ref):
    best = jnp.zeros((1, LANES), F32)
    for r in range(0, refs[0].shape[0], TK):
        xf = sum(ref[r:r + TK, :] for ref in refs).astype(F32)
        n2 = _dot((xf * xf).astype(BF16), gsum_ref[...])
        best = jnp.maximum(best, jnp.max(n2, axis=0, keepdims=True))
    return best


def _logit_bounds(q_half_refs, k_ref, gsum_ref):
    qn = _max_sq_group_norms(q_half_refs, gsum_ref)
    kn = _max_sq_group_norms((k_ref,), gsum_ref)
    return jnp.sqrt(qn * kn) * BOUND_MARGIN


def _lane_pack(column_values):
    lane = lax.broadcasted_iota(jnp.int32, (1, LANES), 1)
    row = jnp.zeros((1, LANES), F32)
    for h, v in enumerate(column_values):
        row = jnp.where(lane == h, v, row)
    return row


def _lane_reduce(row, n, op, fill):
    lane = lax.broadcasted_iota(jnp.int32, (1, LANES), 1)
    return op(jnp.where(lane < n, row, fill))


def _proj_kernel(x_ref, g_ref, w_ref, o_ref, *, row_block):
    n_slabs = w_ref.shape[1] // SLAB
    lane = lax.broadcasted_iota(jnp.int32, (1, SLAB), 1)
    first = lax.bitwise_and(lane, LANES - 1) < HEAD_DIM
    for r in range(0, x_ref.shape[0], row_block):
        rows = slice(r, r + row_block)
        h = _rms(x_ref[rows, :], g_ref[...]).astype(BF16)
        for slab in range(n_slabs):
            cols = slice(slab * SLAB, (slab + 1) * SLAB)
            y = _dot(h, w_ref[:, cols])
            if slab in Q_SLABS:
                y = y * Q_SCALE
                extra = n_slabs + Q_SLABS.index(slab)
                o_ref[rows, extra * SLAB:(extra + 1) * SLAB] = jnp.where(first, 0.0, y).astype(BF16)
                y = jnp.where(first, y, 0.0)
            o_ref[rows, cols] = y.astype(BF16)


def _in_projection(xf, gain, w):
    t, d = xf.shape
    n = w.shape[1]
    n_out = n + len(Q_SLABS) * SLAB
    return pl.pallas_call(
        functools.partial(_proj_kernel, row_block=512),
        grid=(t // TM_PROJ,),
        in_specs=[
            pl.BlockSpec((TM_PROJ, d), lambda i: (i, 0)),
            pl.BlockSpec((1, d), lambda i: (0, 0)),
            pl.BlockSpec((d, n), lambda i: (0, 0), pipeline_mode=pl.Buffered(1)),
        ],
        out_specs=pl.BlockSpec((TM_PROJ, n_out), lambda i: (i, 0)),
        out_shape=jax.ShapeDtypeStruct((t, n_out), BF16),
        compiler_params=pltpu.CompilerParams(
            dimension_semantics=("arbitrary",), vmem_limit_bytes=VMEM_LIMIT),
        name="in_projection",
    )(xf, gain, w)


def _softplus2(z2):
    return jnp.maximum(z2, jnp.log2(1.0 + jnp.exp2(jnp.minimum(z2, SOFTPLUS_CLAMP_LOG2))))


def _sb_kernel(qa_ref, qb_ref, k_ref, v_ref, tri2_ref, gsum_ref, o_ref, zb_ref):
    i = pl.program_id(1)

    @pl.when(i == 0)
    def _():
        zb_ref[...] = _logit_bounds((qa_ref, qb_ref), k_ref, gsum_ref)

    tile = SB_TILE
    q_rows = pl.ds(pl.multiple_of(i * tile, tile), tile)
    q_half_refs = (qa_ref, qb_ref)

    def block(j, carries, diag):
        start = pl.multiple_of(j * tile, tile)
        tri2 = tri2_ref[...]
        if diag:
            row = lax.broadcasted_iota(jnp.int32, (tile, tile), 0)
            col = lax.broadcasted_iota(jnp.int32, (tile, tile), 1)
            strict = col < row
        new_carries, pvs = [], []
        for g in range(N_GROUPS):
            cols = slice(g * LANES, (g + 1) * LANES)
            kb = k_ref[pl.ds(start, tile), cols]
            v_heads = _lane_halves(v_ref[pl.ds(start, tile), cols])
            ws = []
            for sub in range(2):
                h = 2 * g + sub
                z = _nt_dot(q_half_refs[sub][q_rows, cols], kb)
                sp = _softplus2(z)
                if diag:
                    sp = jnp.where(strict, sp, 0.0)
                hi = sp.astype(BF16)
                lo = (sp - hi.astype(F32)).astype(BF16)
                suffix = _dot(jnp.concatenate([hi, lo], axis=1), tri2)
                w = jnp.exp2(z - suffix - carries[h])
                if diag:
                    w = jnp.where(strict, w, 0.0)
                new_carries.append(carries[h] + jnp.sum(sp, axis=1, keepdims=True))
                ws.append(w.astype(BF16))
            pvs.append(_dot(jnp.concatenate(ws, axis=1), jnp.concatenate(v_heads, axis=0)))
        return tuple(new_carries), pvs

    def skip_slack(carries):
        least = _lane_pack([jnp.min(c, axis=0, keepdims=True) for c in carries])
        return _lane_reduce(least - zb_ref[...], SB_HEADS, jnp.min, jnp.inf)

    zeros = (jnp.zeros((tile, 1), F32),) * SB_HEADS

    def start_with(n_previous):
        def run(_):
            carries, total = block(i, zeros, True)
            for t in range(n_previous):
                carries, pvs = block(i - 1 - t, carries, False)
                total = [a + b for a, b in zip(total, pvs)]
            for g in range(N_GROUPS):
                o_ref[:, g * LANES:(g + 1) * LANES] = total[g]
            return (jnp.int32(n_previous), skip_slack(carries)) + carries
        return run

    def start(n):
        if n == SB_MERGED_PREVIOUS:
            return start_with(n)
        return lambda _: lax.cond(i > n, start(n + 1), start_with(n), 0)

    state = start(0)(0)

    def cond(state):
        return jnp.logical_and(state[0] < i, state[1] < SKIP_LOG2)

    def body(state):
        t = state[0]
        carries, pvs = block(i - 1 - t, state[2:], False)
        for g in range(N_GROUPS):
            o_ref[:, g * LANES:(g + 1) * LANES] += pvs[g]
        return (t + 1, skip_slack(carries)) + carries

    lax.while_loop(cond, body, state)


def _sb_attention(proj, tri2, gsum, batch, seq):
    nq = seq // SB_TILE
    t = proj.shape[0]
    return pl.pallas_call(
        _sb_kernel,
        grid=(batch, nq),
        in_specs=[
            pl.BlockSpec((seq, SLAB), lambda b, i: (b, SB_Q_HALVES[0])),
            pl.BlockSpec((seq, SLAB), lambda b, i: (b, SB_Q_HALVES[1])),
            pl.BlockSpec((seq, SLAB), lambda b, i: (b, 1)),
            pl.BlockSpec((seq, SLAB), lambda b, i: (b, 2)),
            pl.BlockSpec((2 * SB_TILE, SB_TILE), lambda b, i: (0, 0)),
            pl.BlockSpec((SLAB, LANES), lambda b, i: (0, 0)),
        ],
        out_specs=pl.BlockSpec((SB_TILE, SLAB), lambda b, i: (b * nq + i, 0)),
        out_shape=jax.ShapeDtypeStruct((t, SLAB), F32),
        scratch_shapes=[pltpu.VMEM((1, LANES), F32)],
        compiler_params=pltpu.CompilerParams(
            dimension_semantics=("arbitrary", "arbitrary"), vmem_limit_bytes=VMEM_LIMIT),
        name="sb_attention",
    )(proj, proj, proj, proj, tri2, gsum)


def _df_kernel(slope_ref, qa_ref, qb_ref, k_ref, v_ref, bias_ref, dbias_ref, lq_ref, lk_ref,
               gsum_ref, o_ref, acc_ref, plan_ref, lb_ref, *, lambda_init):
    i = pl.program_id(1)
    n_chain = 2 * DF_HEADS

    @pl.when(i == 0)
    def _():
        lbound = _logit_bounds((qa_ref, qb_ref), k_ref, gsum_ref)
        lane = lax.broadcasted_iota(jnp.int32, (1, LANES), 1)
        for c in range(n_chain):
            lb_ref[c] = jnp.max(jnp.where(lane == c, lbound, 0.0))
        worst = 2.0 * _lane_reduce(lbound, n_chain, jnp.max, -jnp.inf)
        plan_ref[0] = (worst <= MAX_FIXED_REF_LOG2).astype(jnp.int32)
        reach = (2.0 * _lane_reduce(lbound, 2, jnp.max, -jnp.inf) + SKIP_LOG2) / slope_ref[0]
        kept = jnp.int32(1)
        for t in range(1, k_ref.shape[0] // TK):
            kept += (reach >= t).astype(jnp.int32)
        plan_ref[1] = kept

    fixed_ref_ok = plan_ref[0] == 1
    n_near = jnp.minimum(plan_ref[1], i)

    q_rows = pl.ds(pl.multiple_of(i * TQ, TQ), TQ)
    q_half_refs = (qa_ref, qb_ref)

    def scores(j, c, bias):
        hd = c // 2
        cols = slice(hd * LANES, (hd + 1) * LANES)
        start = pl.multiple_of(j * TK, TK)
        return _nt_dot(q_half_refs[c % 2][q_rows, cols], k_ref[pl.ds(start, TK), cols]) + bias

    ones = jnp.ones((TK, LANES), BF16)

    def values(j, hd):
        vb = v_ref[pl.ds(pl.multiple_of(j * TK, TK), TK), hd * LANES:(hd + 1) * LANES]
        return jnp.concatenate([vb, ones], axis=1)

    def block_shift(t, hd):
        return slope_ref[hd] * jnp.asarray(t + 1).astype(F32)

    def diagonal(c):
        return jnp.exp2(scores(i, c, dbias_ref[c // 2]) + lb_ref[c]).astype(BF16)

    def earlier(t, c):
        hd = c // 2
        offset = lb_ref[c] - block_shift(t, hd)
        return jnp.exp2(scores(i - 1 - t, c, bias_ref[hd]) + offset).astype(BF16)

    def contract(ps, js, hd):
        return _dot(jnp.concatenate(ps, axis=1), jnp.concatenate([values(j, hd) for j in js], axis=0))

    def start_with(n_merged):
        for c in range(n_chain):
            ps = [diagonal(c)] + [earlier(t, c) for t in range(n_merged)]
            acc_ref[c] = contract(ps, [i - t for t in range(n_merged + 1)], c // 2)

    @pl.when(jnp.logical_and(fixed_ref_ok, i == 0))
    def _():
        start_with(0)

    @pl.when(jnp.logical_and(fixed_ref_ok, i == 1))
    def _():
        start_with(1)

    @pl.when(jnp.logical_and(fixed_ref_ok, i >= 2))
    def _():
        start_with(2)

        def add_blocks(ts, chains):
            for c in chains:
                ps = [earlier(t, c) for t in ts]
                acc_ref[c] += contract(ps, [i - 1 - t for t in ts], c // 2)

        first_far = jnp.maximum(n_near, 2)

        def near(t, carry):
            add_blocks((t,), range(n_chain))
            return carry

        lax.fori_loop(2, first_far, near, 0)

        far_chains = range(2, n_chain)
        n_far = i - first_far

        def far_pair(u, carry):
            add_blocks((first_far + 2 * u, first_far + 2 * u + 1), far_chains)
            return carry

        lax.fori_loop(0, lax.shift_right_logical(n_far, 1), far_pair, 0)

        @pl.when(lax.bitwise_and(n_far, 1) == 1)
        def _():
            add_blocks((i - 1,), far_chains)

    @pl.when(jnp.logical_not(fixed_ref_ok))
    def _():
        ms = []
        for c in range(n_chain):
            s = scores(i, c, dbias_ref[c // 2])
            m = jnp.max(s, axis=1, keepdims=True)
            ms.append(m)
            acc_ref[c] = contract([jnp.exp2(s - m).astype(BF16)], [i], c // 2)

        def body(t, m_run):
            j = i - 1 - t
            m_out = []
            for c in range(n_chain):
                hd = c // 2
                shift = block_shift(t, hd)
                s = scores(j, c, bias_ref[hd])
                m_new = jnp.maximum(m_run[c], jnp.max(s, axis=1, keepdims=True) - shift)
                alpha = jnp.exp2(m_run[c] - m_new)
                p = jnp.exp2(s - (m_new + shift))
                m_out.append(m_new)
                acc_ref[c] = alpha * acc_ref[c] + _dot(p.astype(BF16), values(j, hd))
            return tuple(m_out)

        lax.fori_loop(0, i, body, tuple(ms))

    dots = jnp.sum(lq_ref[...] * lk_ref[...], axis=1, keepdims=True)
    lam = jnp.exp(dots[0:1]) - jnp.exp(dots[1:2]) + lambda_init
    for hd in range(DF_HEADS):
        a0, a1 = acc_ref[2 * hd], acc_ref[2 * hd + 1]
        o_ref[:, hd * LANES:(hd + 1) * LANES] = (
            a0[:, :LANES] / a0[:, LANES:] - lam * (a1[:, :LANES] / a1[:, LANES:]))


def _df_attention(proj, slopes, bias, dbias, lq, lk, gsum, batch, seq, lambda_init):
    nq = seq // TQ
    t = proj.shape[0]
    const3 = lambda b, i: (0, 0, 0)
    const2 = lambda b, i: (0, 0)
    return pl.pallas_call(
        functools.partial(_df_kernel, lambda_init=lambda_init),
        grid=(batch, nq),
        in_specs=[
            pl.BlockSpec(memory_space=pltpu.SMEM),
            pl.BlockSpec((seq, SLAB), lambda b, i: (b, DF_Q_HALVES[0])),
            pl.BlockSpec((seq, SLAB), lambda b, i: (b, DF_Q_HALVES[1])),
            pl.BlockSpec((seq, SLAB), lambda b, i: (b, 5)),
            pl.BlockSpec((seq, SLAB), lambda b, i: (b, 6)),
            pl.BlockSpec((DF_HEADS, TQ, TK), const3),
            pl.BlockSpec((DF_HEADS, TQ, TK), const3),
            pl.BlockSpec((2, HEAD_DIM), const2),
            pl.BlockSpec((2, HEAD_DIM), const2),
            pl.BlockSpec((SLAB, LANES), const2),
        ],
        out_specs=pl.BlockSpec((TQ, SLAB), lambda b, i: (b * nq + i, 0)),
        out_shape=jax.ShapeDtypeStruct((t, SLAB), F32),
        scratch_shapes=[pltpu.VMEM((2 * DF_HEADS, TQ, 2 * LANES), F32),
                        pltpu.SMEM((2,), jnp.int32),
                        pltpu.SMEM((2 * DF_HEADS,), F32)],
        compiler_params=pltpu.CompilerParams(
            dimension_semantics=("arbitrary", "arbitrary"), vmem_limit_bytes=VMEM_LIMIT),
        name="df_attention",
    )(slopes, proj, proj, proj, proj, bias, dbias, lq, lk, gsum)


def _out_kernel(x_ref, osb_ref, odf_ref, zsb_ref, zdf_ref, gpre_ref, gpost_ref, ghead_ref,
                wg_ref, bg_ref, wsb_ref, wdf_ref, wo_ref, o_ref, m_ref,
                *, col_chunk, row_block, lambda_init):
    d = x_ref.shape[1]
    for r in range(0, x_ref.shape[0], row_block):
        rows = slice(r, r + row_block)
        x = x_ref[rows, :]
        h = _rms(x, gpre_ref[...]).astype(BF16)
        a_sb = (osb_ref[rows, :] * _silu(zsb_ref[rows, :].astype(F32))).astype(BF16)
        a_df = []
        for hd in range(DF_HEADS):
            cols = slice(hd * LANES, (hd + 1) * LANES)
            o = _rms(odf_ref[rows, cols], ghead_ref[...]) * (1.0 - lambda_init)
            a_df.append((o * _silu(zdf_ref[rows, cols].astype(F32))).astype(BF16))
        a_df = jnp.concatenate(a_df, axis=1)
        for n in range(0, d, col_chunk):
            cols = slice(n, n + col_chunk)
            dcols = slice(d + n, d + n + col_chunk)
            g_sb = jax.nn.sigmoid(_dot(h, wg_ref[:, cols]) + bg_ref[:, cols])
            g_df = jax.nn.sigmoid(_dot(h, wg_ref[:, dcols]) + bg_ref[:, dcols])
            y_sb = _dot(a_sb, wsb_ref[:, cols])
            y_df = _dot(a_df, wdf_ref[:, cols])
            m_ref[rows, cols] = (g_sb * y_sb + g_df * y_df).astype(BF16)
        out = _dot(m_ref[rows, :], wo_ref[...])
        o_ref[rows, :] = x + _rms(out, gpost_ref[...])


def _output_stage(xf, o_sb, o_df, proj, gpre, gpost, ghead, wg, bg, wsb, wdf, wo, lambda_init):
    t, d = xf.shape
    const = lambda i: (0, 0)
    resident = lambda a: pl.BlockSpec(a.shape, const, pipeline_mode=pl.Buffered(1))
    rows = lambda width, col: pl.BlockSpec((TM_OUT, width), lambda i: (i, col))
    return pl.pallas_call(
        functools.partial(_out_kernel, col_chunk=512, row_block=256, lambda_init=lambda_init),
        grid=(t // TM_OUT,),
        in_specs=[
            rows(d, 0),
            rows(SLAB, 0),
            rows(SLAB, 0),
            rows(SLAB, 3),
            rows(SLAB, 7),
            pl.BlockSpec((1, d), const),
            pl.BlockSpec((1, d), const),
            pl.BlockSpec((1, LANES), const),
            resident(wg), pl.BlockSpec(bg.shape, const), resident(wsb), resident(wdf), resident(wo),
        ],
        out_specs=rows(d, 0),
        out_shape=jax.ShapeDtypeStruct((t, d), F32),
        scratch_shapes=[pltpu.VMEM((TM_OUT, d), BF16)],
        compiler_params=pltpu.CompilerParams(
            dimension_semantics=("arbitrary",), vmem_limit_bytes=VMEM_LIMIT),
        name="output_stage",
    )(xf, o_sb, o_df, proj, proj, gpre, gpost, ghead, wg, bg, wsb, wdf, wo)


def _constant_tiles():
    sb = np.arange(SB_TILE)
    tri = (sb[:, None] >= sb[None, :]).astype(np.float32)
    tri2 = np.concatenate([tri, tri], axis=0)
    r = np.arange(TQ)[:, None]
    c = np.arange(TK)[None, :]
    gsum = (np.arange(SLAB)[:, None] // HEAD_DIM == np.arange(LANES)[None, :]).astype(np.float32)
    slopes = np.array([2.0 ** (-8.0 * (h + 1) / DF_HEADS) for h in range(DF_HEADS)],
                      dtype=np.float64) * LOG2E
    rel = (r - c).astype(np.float64)
    bias = -slopes[:, None, None] * rel[None]
    visible = (c // CHUNK) <= (r // CHUNK)
    dbias = np.where(visible[None], -slopes[:, None, None] * np.abs(rel)[None], NEG_BIG)
    block_slopes = slopes * TQ
    return (jnp.asarray(tri2, BF16), jnp.asarray(gsum, BF16), jnp.asarray(block_slopes, F32),
            jnp.asarray(bias, F32), jnp.asarray(dbias, F32))


def kernel(x, norm_pre, norm_post, w_in, w_gate, b_gate, w_branch_sb, w_branch_df, w_out,
           lambda_q, lambda_k, df_head_norm):
    b, s, d = x.shape
    depth = w_in.shape[0]
    assert s % TQ == 0 and TQ == TK and TQ % CHUNK == 0
    assert s % SB_TILE == 0
    assert (b * s) % TM_PROJ == 0 and (b * s) % TM_OUT == 0
    tri2, gsum, block_slopes, bias, dbias = _constant_tiles()
    xf = x.reshape(b * s, d)
    for l in range(depth):
        lambda_init = 0.8 - 0.6 * float(np.exp(-0.3 * l))
        proj = _in_projection(xf, norm_pre[l][None], w_in[l].astype(BF16))
        o_sb = _sb_attention(proj, tri2, gsum, b, s)
        o_df = _df_attention(proj, block_slopes, bias, dbias, lambda_q[l].astype(F32),
                             lambda_k[l].astype(F32), gsum, b, s, lambda_init)
        xf = _output_stage(xf, o_sb, o_df, proj, norm_pre[l][None], norm_post[l][None],
                           df_head_norm[l][None].astype(F32),
                           w_gate[l].astype(BF16), b_gate[l][None],
                           w_branch_sb[l].astype(BF16), w_branch_df[l].astype(BF16),
                           w_out[l].astype(BF16), lambda_init)
    return xf.reshape(b, s, d)
```

```python
import functools
import math

import numpy as np
import jax
import jax.numpy as jnp
from jax import lax
from jax.experimental import pallas as pl
from jax.experimental.pallas import tpu as pltpu

F32 = jnp.float32
BF16 = jnp.bfloat16

EPS = 1e-6
CHUNK = 64
HEAD_DIM = 64
LANES = 128
N_GROUPS = 4
SLAB = N_GROUPS * LANES
SB_HEADS = SLAB // HEAD_DIM
DF_HEADS = 4
LOG2E = math.log2(math.e)
Q_SCALE = HEAD_DIM ** -0.5 * LOG2E
Q_SLABS = (0, 4)
SB_Q_HALVES = (0, 8)
DF_Q_HALVES = (4, 9)

TQ = 256
TK = 256
SB_TILE = 256
SB_PREV_ROWS = 192
TM_PROJ = 1024
TM_OUT = 1024
NEG_BIG = -1e30
SKIP_LOG2 = 135.0
BOUND_MARGIN = 1.01
MAX_FIXED_REF_LOG2 = 100.0
SOFTPLUS_CLAMP_LOG2 = 100.0

VMEM_LIMIT = 56 * 1024 * 1024


def _nt_dot(a, b):
    return lax.dot_general(a, b, (((1,), (1,)), ((), ())), preferred_element_type=F32)


def _dot(a, b):
    return jnp.dot(a, b, preferred_element_type=F32)


def _rms(x, gain):
    return x * lax.rsqrt(jnp.mean(x * x, axis=-1, keepdims=True) + EPS) * gain


def _silu(z):
    return z * jax.nn.sigmoid(z)


def _lane_halves(x):
    lane = lax.broadcasted_iota(jnp.int32, (1, LANES), 1)
    first = lane < HEAD_DIM
    zero = jnp.zeros_like(x)
    return jnp.where(first, x, zero), jnp.where(first, zero, x)


def _max_sq_group_norms(refs, gsum_ref):
    best = jnp.zeros((1, LANES), F32)
    for r in range(0, refs[0].shape[0], TK):
        xf = sum(ref[r:r + TK, :] for ref in refs).astype(F32)
        n2 = _dot((xf * xf).astype(BF16), gsum_ref[...])
        best = jnp.maximum(best, jnp.max(n2, axis=0, keepdims=True))
    return best


def _logit_bounds(q_half_refs, k_ref, gsum_ref):
    qn = _max_sq_group_norms(q_half_refs, gsum_ref)
    kn = _max_sq_group_norms((k_ref,), gsum_ref)
    return jnp.sqrt(qn * kn) * BOUND_MARGIN


def _lane_pack(column_values):
    lane = lax.broadcasted_iota(jnp.int32, (1, LANES), 1)
    row = jnp.zeros((1, LANES), F32)
    for h, v in enumerate(column_values):
        row = jnp.where(lane == h, v, row)
    return row


def _lane_reduce(row, n, op, fill):
    lane = lax.broadcasted_iota(jnp.int32, (1, LANES), 1)
    return op(jnp.where(lane < n, row, fill))


def _proj_kernel(x_ref, g_ref, w_ref, o_ref, *, row_block):
    n_slabs = w_ref.shape[1] // SLAB
    lane = lax.broadcasted_iota(jnp.int32, (1, SLAB), 1)
    first = lax.bitwise_and(lane, LANES - 1) < HEAD_DIM
    for r in range(0, x_ref.shape[0], row_block):
        rows = slice(r, r + row_block)
        h = _rms(x_ref[rows, :], g_ref[...]).astype(BF16)
        for slab in range(n_slabs):
            cols = slice(slab * SLAB, (slab + 1) * SLAB)
            y = _dot(h, w_ref[:, cols])
            if slab in Q_SLABS:
                y = y * Q_SCALE
                extra = n_slabs + Q_SLABS.index(slab)
                o_ref[rows, extra * SLAB:(extra + 1) * SLAB] = jnp.where(first, 0.0, y).astype(BF16)
                y = jnp.where(first, y, 0.0)
            o_ref[rows, cols] = y.astype(BF16)


def _in_projection(xf, gain, w):
    t, d = xf.shape
    n = w.shape[1]
    n_out = n + len(Q_SLABS) * SLAB
    return pl.pallas_call(
        functools.partial(_proj_kernel, row_block=512),
        grid=(t // TM_PROJ,),
        in_specs=[
            pl.BlockSpec((TM_PROJ, d), lambda i: (i, 0)),
            pl.BlockSpec((1, d), lambda i: (0, 0)),
            pl.BlockSpec((d, n), lambda i: (0, 0), pipeline_mode=pl.Buffered(1)),
        ],
        out_specs=pl.BlockSpec((TM_PROJ, n_out), lambda i: (i, 0)),
        out_shape=jax.ShapeDtypeStruct((t, n_out), BF16),
        compiler_params=pltpu.CompilerParams(
            dimension_semantics=("arbitrary",), vmem_limit_bytes=VMEM_LIMIT),
        name="in_projection",
    )(xf, gain, w)


def _softplus2(z2):
    return jnp.maximum(z2, jnp.log2(1.0 + jnp.exp2(jnp.minimum(z2, SOFTPLUS_CLAMP_LOG2))))


def _sb_kernel(qa_ref, qb_ref, k_ref, v_ref, tri2_ref, gsum_ref, o_ref, zb_ref):
    i = pl.program_id(1)

    @pl.when(i == 0)
    def _():
        zb_ref[...] = _logit_bounds((qa_ref, qb_ref), k_ref, gsum_ref)

    tile = SB_TILE
    q_half_refs = (qa_ref, qb_ref)

    def block(j, carries, diag, r0=0, r1=SB_TILE):
        start = pl.multiple_of(j * tile, tile)
        q_rows = pl.ds(pl.multiple_of(i * tile, tile) + r0, r1 - r0)
        tri2 = tri2_ref[...]
        if diag:
            row = lax.broadcasted_iota(jnp.int32, (r1 - r0, tile), 0) + r0
            col = lax.broadcasted_iota(jnp.int32, (r1 - r0, tile), 1)
            strict = col < row
        new_carries, pvs = [], []
        for g in range(N_GROUPS):
            cols = slice(g * LANES, (g + 1) * LANES)
            kb = k_ref[pl.ds(start, tile), cols]
            v_heads = _lane_halves(v_ref[pl.ds(start, tile), cols])
            ws = []
            for sub in range(2):
                h = 2 * g + sub
                z = _nt_dot(q_half_refs[sub][q_rows, cols], kb)
                sp = _softplus2(z)
                if diag:
                    sp = jnp.where(strict, sp, 0.0)
                hi = sp.astype(BF16)
                lo = (sp - hi.astype(F32)).astype(BF16)
                suffix = _dot(jnp.concatenate([hi, lo], axis=1), tri2)
                w = jnp.exp2(z - suffix - carries[h])
                if diag:
                    w = jnp.where(strict, w, 0.0)
                new_carries.append(carries[h] + jnp.sum(sp, axis=1, keepdims=True))
                ws.append(w.astype(BF16))
            pvs.append(_dot(jnp.concatenate(ws, axis=1), jnp.concatenate(v_heads, axis=0)))
        return tuple(new_carries), pvs

    def skip_slack(carries):
        least = _lane_pack([jnp.min(c, axis=0, keepdims=True) for c in carries])
        return _lane_reduce(least - zb_ref[...], SB_HEADS, jnp.min, jnp.inf)

    zeros = (jnp.zeros((tile, 1), F32),) * SB_HEADS

    def diagonal_only(_):
        carries, total = block(i, zeros, True)
        for g in range(N_GROUPS):
            o_ref[:, g * LANES:(g + 1) * LANES] = total[g]
        return (jnp.int32(0), skip_slack(carries)) + carries

    def diagonal_and_previous(_):
        p = SB_PREV_ROWS
        carries, total = block(i, zeros, True)
        top, pvs = block(i - 1, tuple(c[:p] for c in carries), False, 0, p)
        for g in range(N_GROUPS):
            cols = slice(g * LANES, (g + 1) * LANES)
            o_ref[:p, cols] = total[g][:p] + pvs[g]
            o_ref[p:, cols] = total[g][p:]
        rest = tuple(c[p:] for c in carries)
        rest_slack = skip_slack(rest)

        def previous_for_rest(_):
            new_rest, pvs = block(i - 1, rest, False, p, tile)
            for g in range(N_GROUPS):
                o_ref[p:, g * LANES:(g + 1) * LANES] += pvs[g]
            return (skip_slack(new_rest),) + new_rest

        fixed = lax.cond(rest_slack < SKIP_LOG2, previous_for_rest,
                         lambda _: (rest_slack,) + rest, 0)
        carries = tuple(jnp.concatenate([a, b], axis=0) for a, b in zip(top, fixed[1:]))
        return (jnp.int32(1), jnp.minimum(skip_slack(top), fixed[0])) + carries

    state = lax.cond(i > 0, diagonal_and_previous, diagonal_only, 0)

    def cond(state):
        return jnp.logical_and(state[0] < i, state[1] < SKIP_LOG2)

    def body(state):
        t = state[0]
        carries, pvs = block(i - 1 - t, state[2:], False)
        for g in range(N_GROUPS):
            o_ref[:, g * LANES:(g + 1) * LANES] += pvs[g]
        return (t + 1, skip_slack(carries)) + carries

    lax.while_loop(cond, body, state)


def _sb_attention(proj, tri2, gsum, batch, seq):
    nq = seq // SB_TILE
    t = proj.shape[0]
    return pl.pallas_call(
        _sb_kernel,
        grid=(batch, nq),
        in_specs=[
            pl.BlockSpec((seq, SLAB), lambda b, i: (b, SB_Q_HALVES[0])),
            pl.BlockSpec((seq, SLAB), lambda b, i: (b, SB_Q_HALVES[1])),
            pl.BlockSpec((seq, SLAB), lambda b, i: (b, 1)),
            pl.BlockSpec((seq, SLAB), lambda b, i: (b, 2)),
            pl.BlockSpec((2 * SB_TILE, SB_TILE), lambda b, i: (0, 0)),
            pl.BlockSpec((SLAB, LANES), lambda b, i: (0, 0)),
        ],
        out_specs=pl.BlockSpec((SB_TILE, SLAB), lambda b, i: (b * nq + i, 0)),
        out_shape=jax.ShapeDtypeStruct((t, SLAB), F32),
        scratch_shapes=[pltpu.VMEM((1, LANES), F32)],
        compiler_params=pltpu.CompilerParams(
            dimension_semantics=("arbitrary", "arbitrary"), vmem_limit_bytes=VMEM_LIMIT),
        name="sb_attention",
    )(proj, proj, proj, proj, tri2, gsum)


def _df_kernel(slope_ref, qa_ref, qb_ref, k_ref, v_ref, bias_ref, dbias_ref, lq_ref, lk_ref,
               gsum_ref, o_ref, acc_ref, plan_ref, lb_ref, *, lambda_init):
    i = pl.program_id(1)
    n_chain = 2 * DF_HEADS

    @pl.when(i == 0)
    def _():
        lbound = _logit_bounds((qa_ref, qb_ref), k_ref, gsum_ref)
        lane = lax.broadcasted_iota(jnp.int32, (1, LANES), 1)
        for c in range(n_chain):
            lb_ref[c] = jnp.max(jnp.where(lane == c, lbound, 0.0))
        worst = 2.0 * _lane_reduce(lbound, n_chain, jnp.max, -jnp.inf)
        plan_ref[0] = (worst <= MAX_FIXED_REF_LOG2).astype(jnp.int32)
        reach = (2.0 * _lane_reduce(lbound, 2, jnp.max, -jnp.inf) + SKIP_LOG2) / slope_ref[0]
        kept = jnp.int32(1)
        for t in range(1, k_ref.shape[0] // TK):
            kept += (reach >= t).astype(jnp.int32)
        plan_ref[1] = kept

    fixed_ref_ok = plan_ref[0] == 1
    n_near = jnp.minimum(plan_ref[1], i)

    q_rows = pl.ds(pl.multiple_of(i * TQ, TQ), TQ)
    q_half_refs = (qa_ref, qb_ref)

    def scores(j, c, bias):
        hd = c // 2
        cols = slice(hd * LANES, (hd + 1) * LANES)
        start = pl.multiple_of(j * TK, TK)
        return _nt_dot(q_half_refs[c % 2][q_rows, cols], k_ref[pl.ds(start, TK), cols]) + bias

    ones = jnp.ones((TK, LANES), BF16)

    def values(j, hd):
        vb = v_ref[pl.ds(pl.multiple_of(j * TK, TK), TK), hd * LANES:(hd + 1) * LANES]
        return jnp.concatenate([vb, ones], axis=1)

    def block_shift(t, hd):
        return slope_ref[hd] * jnp.asarray(t + 1).astype(F32)

    def diagonal(c):
        return jnp.exp2(scores(i, c, dbias_ref[c // 2]) + lb_ref[c]).astype(BF16)

    def earlier(t, c):
        hd = c // 2
        offset = lb_ref[c] - block_shift(t, hd)
        return jnp.exp2(scores(i - 1 - t, c, bias_ref[hd]) + offset).astype(BF16)

    def contract(ps, js, hd):
        return _dot(jnp.concatenate(ps, axis=1), jnp.concatenate([values(j, hd) for j in js], axis=0))

    def start_with(n_merged):
        for c in range(n_chain):
            ps = [diagonal(c)] + [earlier(t, c) for t in range(n_merged)]
            acc_ref[c] = contract(ps, [i - t for t in range(n_merged + 1)], c // 2)

    @pl.when(jnp.logical_and(fixed_ref_ok, i == 0))
    def _():
        start_with(0)

    @pl.when(jnp.logical_and(fixed_ref_ok, i == 1))
    def _():
        start_with(1)

    @pl.when(jnp.logical_and(fixed_ref_ok, i >= 2))
    def _():
        start_with(2)

        def add_blocks(ts, chains):
            for c in chains:
                ps = [earlier(t, c) for t in ts]
                acc_ref[c] += contract(ps, [i - 1 - t for t in ts], c // 2)

        first_far = jnp.maximum(n_near, 2)

        def near(t, carry):
            add_blocks((t,), range(n_chain))
            return carry

        lax.fori_loop(2, first_far, near, 0)

        far_chains = range(2, n_chain)
        n_far = i - first_far

        def far_pair(u, carry):
            add_blocks((first_far + 2 * u, first_far + 2 * u + 1), far_chains)
            return carry

        lax.fori_loop(0, lax.shift_right_logical(n_far, 1), far_pair, 0)

        @pl.when(lax.bitwise_and(n_far, 1) == 1)
        def _():
            add_blocks((i - 1,), far_chains)

    @pl.when(jnp.logical_not(fixed_ref_ok))
    def _():
        ms = []
        for c in range(n_chain):
            s = scores(i, c, dbias_ref[c // 2])
            m = jnp.max(s, axis=1, keepdims=True)
            ms.append(m)
            acc_ref[c] = contract([jnp.exp2(s - m).astype(BF16)], [i], c // 2)

        def body(t, m_run):
            j = i - 1 - t
            m_out = []
            for c in range(n_chain):
                hd = c // 2
                shift = block_shift(t, hd)
                s = scores(j, c, bias_ref[hd])
                m_new = jnp.maximum(m_run[c], jnp.max(s, axis=1, keepdims=True) - shift)
                alpha = jnp.exp2(m_run[c] - m_new)
                p = jnp.exp2(s - (m_new + shift))
                m_out.append(m_new)
                acc_ref[c] = alpha * acc_ref[c] + _dot(p.astype(BF16), values(j, hd))
            return tuple(m_out)

        lax.fori_loop(0, i, body, tuple(ms))

    dots = jnp.sum(lq_ref[...] * lk_ref[...], axis=1, keepdims=True)
    lam = jnp.exp(dots[0:1]) - jnp.exp(dots[1:2]) + lambda_init
    for hd in range(DF_HEADS):
        a0, a1 = acc_ref[2 * hd], acc_ref[2 * hd + 1]
        o_ref[:, hd * LANES:(hd + 1) * LANES] = (
            a0[:, :LANES] / a0[:, LANES:] - lam * (a1[:, :LANES] / a1[:, LANES:]))


def _df_attention(proj, slopes, bias, dbias, lq, lk, gsum, batch, seq, lambda_init):
    nq = seq // TQ
    t = proj.shape[0]
    const3 = lambda b, i: (0, 0, 0)
    const2 = lambda b, i: (0, 0)
    return pl.pallas_call(
        functools.partial(_df_kernel, lambda_init=lambda_init),
        grid=(batch, nq),
        in_specs=[
            pl.BlockSpec(memory_space=pltpu.SMEM),
            pl.BlockSpec((seq, SLAB), lambda b, i: (b, DF_Q_HALVES[0])),
            pl.BlockSpec((seq, SLAB), lambda b, i: (b, DF_Q_HALVES[1])),
            pl.BlockSpec((seq, SLAB), lambda b, i: (b, 5)),
            pl.BlockSpec((seq, SLAB), lambda b, i: (b, 6)),
            pl.BlockSpec((DF_HEADS, TQ, TK), const3),
            pl.BlockSpec((DF_HEADS, TQ, TK), const3),
            pl.BlockSpec((2, HEAD_DIM), const2),
            pl.BlockSpec((2, HEAD_DIM), const2),
            pl.BlockSpec((SLAB, LANES), const2),
        ],
        out_specs=pl.BlockSpec((TQ, SLAB), lambda b, i: (b * nq + i, 0)),
        out_shape=jax.ShapeDtypeStruct((t, SLAB), F32),
        scratch_shapes=[pltpu.VMEM((2 * DF_HEADS, TQ, 2 * LANES), F32),
                        pltpu.SMEM((2,), jnp.int32),
                        pltpu.SMEM((2 * DF_HEADS,), F32)],
        compiler_params=pltpu.CompilerParams(
            dimension_semantics=("arbitrary", "arbitrary"), vmem_limit_bytes=VMEM_LIMIT),
        name="df_attention",
    )(slopes, proj, proj, proj, proj, bias, dbias, lq, lk, gsum)


def _out_kernel(x_ref, osb_ref, odf_ref, zsb_ref, zdf_ref, gpre_ref, gpost_ref, ghead_ref,
                wg_ref, bg_ref, wsb_ref, wdf_ref, wo_ref, o_ref, m_ref,
                *, col_chunk, row_block, lambda_init):
    d = x_ref.shape[1]
    for r in range(0, x_ref.shape[0], row_block):
        rows = slice(r, r + row_block)
        x = x_ref[rows, :]
        h = _rms(x, gpre_ref[...]).astype(BF16)
        a_sb = (osb_ref[rows, :] * _silu(zsb_ref[rows, :].astype(F32))).astype(BF16)
        a_df = []
        for hd in range(DF_HEADS):
            cols = slice(hd * LANES, (hd + 1) * LANES)
            o = _rms(odf_ref[rows, cols], ghead_ref[...]) * (1.0 - lambda_init)
            a_df.append((o * _silu(zdf_ref[rows, cols].astype(F32))).astype(BF16))
        a_df = jnp.concatenate(a_df, axis=1)
        for n in range(0, d, col_chunk):
            cols = slice(n, n + col_chunk)
            dcols = slice(d + n, d + n + col_chunk)
            g_sb = jax.nn.sigmoid(_dot(h, wg_ref[:, cols]) + bg_ref[:, cols])
            g_df = jax.nn.sigmoid(_dot(h, wg_ref[:, dcols]) + bg_ref[:, dcols])
            y_sb = _dot(a_sb, wsb_ref[:, cols])
            y_df = _dot(a_df, wdf_ref[:, cols])
            m_ref[rows, cols] = (g_sb * y_sb + g_df * y_df).astype(BF16)
        out = _dot(m_ref[rows, :], wo_ref[...])
        o_ref[rows, :] = x + _rms(out, gpost_ref[...])


def _output_stage(xf, o_sb, o_df, proj, gpre, gpost, ghead, wg, bg, wsb, wdf, wo, lambda_init):
    t, d = xf.shape
    const = lambda i: (0, 0)
    resident = lambda a: pl.BlockSpec(a.shape, const, pipeline_mode=pl.Buffered(1))
    rows = lambda width, col: pl.BlockSpec((TM_OUT, width), lambda i: (i, col))
    return pl.pallas_call(
        functools.partial(_out_kernel, col_chunk=512, row_block=256, lambda_init=lambda_init),
        grid=(t // TM_OUT,),
        in_specs=[
            rows(d, 0),
            rows(SLAB, 0),
            rows(SLAB, 0),
            rows(SLAB, 3),
            rows(SLAB, 7),
            pl.BlockSpec((1, d), const),
            pl.BlockSpec((1, d), const),
            pl.BlockSpec((1, LANES), const),
            resident(wg), pl.BlockSpec(bg.shape, const), resident(wsb), resident(wdf), resident(wo),
        ],
        out_specs=rows(d, 0),
        out_shape=jax.ShapeDtypeStruct((t, d), F32),
        scratch_shapes=[pltpu.VMEM((TM_OUT, d), BF16)],
        compiler_params=pltpu.CompilerParams(
            dimension_semantics=("arbitrary",), vmem_limit_bytes=VMEM_LIMIT),
        name="output_stage",
    )(xf, o_sb, o_df, proj, proj, gpre, gpost, ghead, wg, bg, wsb, wdf, wo)


def _constant_tiles():
    sb = np.arange(SB_TILE)
    tri = (sb[:, None] >= sb[None, :]).astype(np.float32)
    tri2 = np.concatenate([tri, tri], axis=0)
    r = np.arange(TQ)[:, None]
    c = np.arange(TK)[None, :]
    gsum = (np.arange(SLAB)[:, None] // HEAD_DIM == np.arange(LANES)[None, :]).astype(np.float32)
    slopes = np.array([2.0 ** (-8.0 * (h + 1) / DF_HEADS) for h in range(DF_HEADS)],
                      dtype=np.float64) * LOG2E
    rel = (r - c).astype(np.float64)
    bias = -slopes[:, None, None] * rel[None]
    visible = (c // CHUNK) <= (r // CHUNK)
    dbias = np.where(visible[None], -slopes[:, None, None] * np.abs(rel)[None], NEG_BIG)
    block_slopes = slopes * TQ
    return (jnp.asarray(tri2, BF16), jnp.asarray(gsum, BF16), jnp.asarray(block_slopes, F32),
            jnp.asarray(bias, F32), jnp.asarray(dbias, F32))


def kernel(x, norm_pre, norm_post, w_in, w_gate, b_gate, w_branch_sb, w_branch_df, w_out,
           lambda_q, lambda_k, df_head_norm):
    b, s, d = x.shape
    depth = w_in.shape[0]
    assert s % TQ == 0 and TQ == TK and TQ % CHUNK == 0
    assert s % SB_TILE == 0
    assert (b * s) % TM_PROJ == 0 and (b * s) % TM_OUT == 0
    tri2, gsum, block_slopes, bias, dbias = _constant_tiles()
    xf = x.reshape(b * s, d)
    for l in range(depth):
        lambda_init = 0.8 - 0.6 * float(np.exp(-0.3 * l))
        proj = _in_projection(xf, norm_pre[l][None], w_in[l].astype(BF16))
        o_sb = _sb_attention(proj, tri2, gsum, b, s)
        o_df = _df_attention(proj, block_slopes, bias, dbias, lambda_q[l].astype(F32),
                             lambda_k[l].astype(F32), gsum, b, s, lambda_init)
        xf = _output_stage(xf, o_sb, o_df, proj, norm_pre[l][None], norm_post[l][None],
                           df_head_norm[l][None].astype(F32),
                           w_gate[l].astype(BF16), b_gate[l][None],
                           w_branch_sb[l].astype(BF16), w_branch_df[l].astype(BF16),
                           w_out[l].astype(BF16), lambda_init)
    return xf.reshape(b, s, d)
```

```python
import functools
import math

import numpy as np
import jax
import jax.numpy as jnp
from jax import lax
from jax.experimental import pallas as pl
from jax.experimental.pallas import tpu as pltpu

F32 = jnp.float32
BF16 = jnp.bfloat16

EPS = 1e-6
CHUNK = 64
HEAD_DIM = 64
LANES = 128
N_GROUPS = 4
SLAB = N_GROUPS * LANES
SB_HEADS = SLAB // HEAD_DIM
DF_HEADS = 4
LOG2E = math.log2(math.e)
Q_SCALE = HEAD_DIM ** -0.5 * LOG2E
Q_SLABS = (0, 4)
SB_Q_HALVES = (0, 8)
DF_Q_HALVES = (4, 9)

TQ = 256
TK = 256
SB_TILE = 256
SB_PREV_ROWS = 192
TM_PROJ = 1024
TM_OUT = 1024
NEG_BIG = -1e30
SKIP_LOG2 = 135.0
BOUND_MARGIN = 1.01
MAX_FIXED_REF_LOG2 = 100.0
SOFTPLUS_CLAMP_LOG2 = 100.0

VMEM_LIMIT = 56 * 1024 * 1024


def _nt_dot(a, b):
    return lax.dot_general(a, b, (((1,), (1,)), ((), ())), preferred_element_type=F32)


def _dot(a, b):
    return jnp.dot(a, b, preferred_element_type=F32)


def _rms(x, gain):
    return x * lax.rsqrt(jnp.mean(x * x, axis=-1, keepdims=True) + EPS) * gain


def _silu(z):
    return z * jax.nn.sigmoid(z)


def _lane_halves(x):
    lane = lax.broadcasted_iota(jnp.int32, (1, LANES), 1)
    first = lane < HEAD_DIM
    zero = jnp.zeros_like(x)
    return jnp.where(first, x, zero), jnp.where(first, zero, x)


def _max_sq_group_norms(refs, gsum_ref):
    best = jnp.zeros((1, LANES), F32)
    for r in range(0, refs[0].shape[0], TK):
        xf = sum(ref[r:r + TK, :] for ref in refs).astype(F32)
        n2 = _dot((xf * xf).astype(BF16), gsum_ref[...])
        best = jnp.maximum(best, jnp.max(n2, axis=0, keepdims=True))
    return best


def _logit_bounds(q_half_refs, k_ref, gsum_ref):
    qn = _max_sq_group_norms(q_half_refs, gsum_ref)
    kn = _max_sq_group_norms((k_ref,), gsum_ref)
    return jnp.sqrt(qn * kn) * BOUND_MARGIN


def _lane_pack(column_values):
    lane = lax.broadcasted_iota(jnp.int32, (1, LANES), 1)
    row = jnp.zeros((1, LANES), F32)
    for h, v in enumerate(column_values):
        row = jnp.where(lane == h, v, row)
    return row


def _lane_reduce(row, n, op, fill):
    lane = lax.broadcasted_iota(jnp.int32, (1, LANES), 1)
    return op(jnp.where(lane < n, row, fill))


def _proj_kernel(x_ref, g_ref, w_ref, o_ref, *, row_block):
    n_slabs = w_ref.shape[1] // SLAB
    lane = lax.broadcasted_iota(jnp.int32, (1, SLAB), 1)
    first = lax.bitwise_and(lane, LANES - 1) < HEAD_DIM
    for r in range(0, x_ref.shape[0], row_block):
        rows = slice(r, r + row_block)
        h = _rms(x_ref[rows, :], g_ref[...]).astype(BF16)
        for slab in range(n_slabs):
            cols = slice(slab * SLAB, (slab + 1) * SLAB)
            y = _dot(h, w_ref[:, cols])
            if slab in Q_SLABS:
                y = y * Q_SCALE
                extra = n_slabs + Q_SLABS.index(slab)
                o_ref[rows, extra * SLAB:(extra + 1) * SLAB] = jnp.where(first, 0.0, y).astype(BF16)
                y = jnp.where(first, y, 0.0)
            o_ref[rows, cols] = y.astype(BF16)


def _in_projection(xf, gain, w):
    t, d = xf.shape
    n = w.shape[1]
    n_out = n + len(Q_SLABS) * SLAB
    return pl.pallas_call(
        functools.partial(_proj_kernel, row_block=512),
        grid=(t // TM_PROJ,),
        in_specs=[
            pl.BlockSpec((TM_PROJ, d), lambda i: (i, 0)),
            pl.BlockSpec((1, d), lambda i: (0, 0)),
            pl.BlockSpec((d, n), lambda i: (0, 0), pipeline_mode=pl.Buffered(1)),
        ],
        out_specs=pl.BlockSpec((TM_PROJ, n_out), lambda i: (i, 0)),
        out_shape=jax.ShapeDtypeStruct((t, n_out), BF16),
        compiler_params=pltpu.CompilerParams(
            dimension_semantics=("arbitrary",), vmem_limit_bytes=VMEM_LIMIT),
        name="in_projection",
    )(xf, gain, w)


def _softplus2(z2):
    return jnp.maximum(z2, jnp.log2(1.0 + jnp.exp2(jnp.minimum(z2, SOFTPLUS_CLAMP_LOG2))))


def _sb_kernel(qa_ref, qb_ref, k_ref, v_ref, tri2_ref, gsum_ref, o_ref, zb_ref):
    i = pl.program_id(1)

    @pl.when(i == 0)
    def _():
        zb_ref[...] = _logit_bounds((qa_ref, qb_ref), k_ref, gsum_ref)

    tile = SB_TILE
    q_half_refs = (qa_ref, qb_ref)

    def block(j, carries, diag, r0=0, r1=SB_TILE):
        start = pl.multiple_of(j * tile, tile)
        q_rows = pl.ds(pl.multiple_of(i * tile, tile) + r0, r1 - r0)
        tri2 = tri2_ref[...]
        if diag:
            row = lax.broadcasted_iota(jnp.int32, (r1 - r0, tile), 0) + r0
            col = lax.broadcasted_iota(jnp.int32, (r1 - r0, tile), 1)
            strict = col < row
        new_carries, pvs = [], []
        for g in range(N_GROUPS):
            cols = slice(g * LANES, (g + 1) * LANES)
            kb = k_ref[pl.ds(start, tile), cols]
            v_heads = _lane_halves(v_ref[pl.ds(start, tile), cols])
            ws = []
            for sub in range(2):
                h = 2 * g + sub
                z = _nt_dot(q_half_refs[sub][q_rows, cols], kb)
                sp = _softplus2(z)
                if diag:
                    sp = jnp.where(strict, sp, 0.0)
                hi = sp.astype(BF16)
                lo = (sp - hi.astype(F32)).astype(BF16)
                suffix = _dot(jnp.concatenate([hi, lo], axis=1), tri2)
                w = jnp.exp2(z - suffix - carries[h])
                if diag:
                    w = jnp.where(strict, w, 0.0)
                new_carries.append(carries[h] + jnp.sum(sp, axis=1, keepdims=True))
                ws.append(w.astype(BF16))
            pvs.append(_dot(jnp.concatenate(ws, axis=1), jnp.concatenate(v_heads, axis=0)))
        return tuple(new_carries), pvs

    def skip_slack(carries):
        least = _lane_pack([jnp.min(c, axis=0, keepdims=True) for c in carries])
        return _lane_reduce(least - zb_ref[...], SB_HEADS, jnp.min, jnp.inf)

    zeros = (jnp.zeros((tile, 1), F32),) * SB_HEADS

    def diagonal_only(_):
        carries, total = block(i, zeros, True)
        for g in range(N_GROUPS):
            o_ref[:, g * LANES:(g + 1) * LANES] = total[g]
        return (jnp.int32(0), skip_slack(carries)) + carries

    def diagonal_and_previous(_):
        p = SB_PREV_ROWS
        carries, total = block(i, zeros, True)
        top, pvs = block(i - 1, tuple(c[:p] for c in carries), False, 0, p)
        for g in range(N_GROUPS):
            cols = slice(g * LANES, (g + 1) * LANES)
            o_ref[:p, cols] = total[g][:p] + pvs[g]
            o_ref[p:, cols] = total[g][p:]
        rest = tuple(c[p:] for c in carries)
        rest_slack = skip_slack(rest)

        def previous_for_rest(_):
            new_rest, pvs = block(i - 1, rest, False, p, tile)
            for g in range(N_GROUPS):
                o_ref[p:, g * LANES:(g + 1) * LANES] += pvs[g]
            return (skip_slack(new_rest),) + new_rest

        fixed = lax.cond(rest_slack < SKIP_LOG2, previous_for_rest,
                         lambda _: (rest_slack,) + rest, 0)
        carries = tuple(jnp.concatenate([a, b], axis=0) for a, b in zip(top, fixed[1:]))
        return (jnp.int32(1), jnp.minimum(skip_slack(top), fixed[0])) + carries

    state = lax.cond(i > 0, diagonal_and_previous, diagonal_only, 0)

    def cond(state):
        return jnp.logical_and(state[0] < i, state[1] < SKIP_LOG2)

    def body(state):
        t = state[0]
        carries, pvs = block(i - 1 - t, state[2:], False)
        for g in range(N_GROUPS):
            o_ref[:, g * LANES:(g + 1) * LANES] += pvs[g]
        return (t + 1, skip_slack(carries)) + carries

    lax.while_loop(cond, body, state)


def _sb_attention(proj, tri2, gsum, batch, seq):
    nq = seq // SB_TILE
    t = proj.shape[0]
    return pl.pallas_call(
        _sb_kernel,
        grid=(batch, nq),
        in_specs=[
            pl.BlockSpec((seq, SLAB), lambda b, i: (b, SB_Q_HALVES[0])),
            pl.BlockSpec((seq, SLAB), lambda b, i: (b, SB_Q_HALVES[1])),
            pl.BlockSpec((seq, SLAB), lambda b, i: (b, 1)),
            pl.BlockSpec((seq, SLAB), lambda b, i: (b, 2)),
            pl.BlockSpec((2 * SB_TILE, SB_TILE), lambda b, i: (0, 0)),
            pl.BlockSpec((SLAB, LANES), lambda b, i: (0, 0)),
        ],
        out_specs=pl.BlockSpec((SB_TILE, SLAB), lambda b, i: (b * nq + i, 0)),
        out_shape=jax.ShapeDtypeStruct((t, SLAB), F32),
        scratch_shapes=[pltpu.VMEM((1, LANES), F32)],
        compiler_params=pltpu.CompilerParams(
            dimension_semantics=("arbitrary", "arbitrary"), vmem_limit_bytes=VMEM_LIMIT),
        name="sb_attention",
    )(proj, proj, proj, proj, tri2, gsum)


def _df_kernel(slope_ref, qa_ref, qb_ref, k_ref, v_ref, bias_ref, dbias_ref, lq_ref, lk_ref,
               gsum_ref, o_ref, acc_ref, plan_ref, lb_ref, *, lambda_init):
    i = pl.program_id(1)
    n_chain = 2 * DF_HEADS

    @pl.when(i == 0)
    def _():
        lbound = _logit_bounds((qa_ref, qb_ref), k_ref, gsum_ref)
        lane = lax.broadcasted_iota(jnp.int32, (1, LANES), 1)
        for c in range(n_chain):
            lb_ref[c] = jnp.max(jnp.where(lane == c, lbound, 0.0))
        worst = 2.0 * _lane_reduce(lbound, n_chain, jnp.max, -jnp.inf)
        plan_ref[0] = (worst <= MAX_FIXED_REF_LOG2).astype(jnp.int32)
        reach = (2.0 * _lane_reduce(lbound, 2, jnp.max, -jnp.inf) + SKIP_LOG2) / slope_ref[0]
        kept = jnp.int32(1)
        for t in range(1, k_ref.shape[0] // TK):
            kept += (reach >= t).astype(jnp.int32)
        plan_ref[1] = kept

    fixed_ref_ok = plan_ref[0] == 1
    n_near = jnp.minimum(plan_ref[1], i)

    q_rows = pl.ds(pl.multiple_of(i * TQ, TQ), TQ)
    q_half_refs = (qa_ref, qb_ref)

    def scores(j, c, bias):
        hd = c // 2
        cols = slice(hd * LANES, (hd + 1) * LANES)
        start = pl.multiple_of(j * TK, TK)
        return _nt_dot(q_half_refs[c % 2][q_rows, cols], k_ref[pl.ds(start, TK), cols]) + bias

    ones = jnp.ones((TK, LANES), BF16)

    def values(j, hd):
        vb = v_ref[pl.ds(pl.multiple_of(j * TK, TK), TK), hd * LANES:(hd + 1) * LANES]
        return jnp.concatenate([vb, ones], axis=1)

    def block_shift(t, hd):
        return slope_ref[hd] * jnp.asarray(t + 1).astype(F32)

    def diagonal(c):
        return jnp.exp2(scores(i, c, dbias_ref[c // 2]) + lb_ref[c]).astype(BF16)

    def earlier(t, c):
        hd = c // 2
        offset = lb_ref[c] - block_shift(t, hd)
        return jnp.exp2(scores(i - 1 - t, c, bias_ref[hd]) + offset).astype(BF16)

    def contract(ps, js, hd):
        return _dot(jnp.concatenate(ps, axis=1), jnp.concatenate([values(j, hd) for j in js], axis=0))

    def start_with(n_merged):
        for c in range(n_chain):
            ps = [diagonal(c)] + [earlier(t, c) for t in range(n_merged)]
            acc_ref[c] = contract(ps, [i - t for t in range(n_merged + 1)], c // 2)

    @pl.when(jnp.logical_and(fixed_ref_ok, i == 0))
    def _():
        start_with(0)

    @pl.when(jnp.logical_and(fixed_ref_ok, i == 1))
    def _():
        start_with(1)

    @pl.when(jnp.logical_and(fixed_ref_ok, i >= 2))
    def _():
        start_with(2)

        def add_blocks(ts, chains):
            for c in chains:
                ps = [earlier(t, c) for t in ts]
                acc_ref[c] += contract(ps, [i - 1 - t for t in ts], c // 2)

        first_far = jnp.maximum(n_near, 2)

        def near(t, carry):
            add_blocks((t,), range(n_chain))
            return carry

        lax.fori_loop(2, first_far, near, 0)

        far_chains = range(2, n_chain)
        n_far = i - first_far

        def far_pair(u, carry):
            add_blocks((first_far + 2 * u, first_far + 2 * u + 1), far_chains)
            return carry

        lax.fori_loop(0, lax.shift_right_logical(n_far, 1), far_pair, 0)

        @pl.when(lax.bitwise_and(n_far, 1) == 1)
        def _():
            add_blocks((i - 1,), far_chains)

    @pl.when(jnp.logical_not(fixed_ref_ok))
    def _():
        ms = []
        for c in range(n_chain):
            s = scores(i, c, dbias_ref[c // 2])
            m = jnp.max(s, axis=1, keepdims=True)
            ms.append(m)
            acc_ref[c] = contract([jnp.exp2(s - m).astype(BF16)], [i], c // 2)

        def body(t, m_run):
            j = i - 1 - t
            m_out = []
            for c in range(n_chain):
                hd = c // 2
                shift = block_shift(t, hd)
                s = scores(j, c, bias_ref[hd])
                m_new = jnp.maximum(m_run[c], jnp.max(s, axis=1, keepdims=True) - shift)
                alpha = jnp.exp2(m_run[c] - m_new)
                p = jnp.exp2(s - (m_new + shift))
                m_out.append(m_new)
                acc_ref[c] = alpha * acc_ref[c] + _dot(p.astype(BF16), values(j, hd))
            return tuple(m_out)

        lax.fori_loop(0, i, body, tuple(ms))

    dots = jnp.sum(lq_ref[...] * lk_ref[...], axis=1, keepdims=True)
    lam = jnp.exp(dots[0:1]) - jnp.exp(dots[1:2]) + lambda_init
    for hd in range(DF_HEADS):
        a0, a1 = acc_ref[2 * hd], acc_ref[2 * hd + 1]
        o_ref[:, hd * LANES:(hd + 1) * LANES] = (
            a0[:, :LANES] / a0[:, LANES:] - lam * (a1[:, :LANES] / a1[:, LANES:]))


def _df_attention(proj, slopes, bias, dbias, lq, lk, gsum, batch, seq, lambda_init):
    nq = seq // TQ
    t = proj.shape[0]
    const3 = lambda b, i: (0, 0, 0)
    const2 = lambda b, i: (0, 0)
    return pl.pallas_call(
        functools.partial(_df_kernel, lambda_init=lambda_init),
        grid=(batch, nq),
        in_specs=[
            pl.BlockSpec(memory_space=pltpu.SMEM),
            pl.BlockSpec((seq, SLAB), lambda b, i: (b, DF_Q_HALVES[0])),
            pl.BlockSpec((seq, SLAB), lambda b, i: (b, DF_Q_HALVES[1])),
            pl.BlockSpec((seq, SLAB), lambda b, i: (b, 5)),
            pl.BlockSpec((seq, SLAB), lambda b, i: (b, 6)),
            pl.BlockSpec((DF_HEADS, TQ, TK), const3),
            pl.BlockSpec((DF_HEADS, TQ, TK), const3),
            pl.BlockSpec((2, HEAD_DIM), const2),
            pl.BlockSpec((2, HEAD_DIM), const2),
            pl.BlockSpec((SLAB, LANES), const2),
        ],
        out_specs=pl.BlockSpec((TQ, SLAB), lambda b, i: (b * nq + i, 0)),
        out_shape=jax.ShapeDtypeStruct((t, SLAB), F32),
        scratch_shapes=[pltpu.VMEM((2 * DF_HEADS, TQ, 2 * LANES), F32),
                        pltpu.SMEM((2,), jnp.int32),
                        pltpu.SMEM((2 * DF_HEADS,), F32)],
        compiler_params=pltpu.CompilerParams(
            dimension_semantics=("arbitrary", "arbitrary"), vmem_limit_bytes=VMEM_LIMIT),
        name="df_attention",
    )(slopes, proj, proj, proj, proj, bias, dbias, lq, lk, gsum)


def _attention_kernel(slope_ref, sqa_ref, sqb_ref, sk_ref, sv_ref, tri2_ref,
                      dqa_ref, dqb_ref, dk_ref, dv_ref, bias_ref, dbias_ref, lq_ref, lk_ref,
                      gsum_ref, osb_ref, odf_ref, zb_ref, acc_ref, plan_ref, lb_ref,
                      *, lambda_init):
    i = pl.program_id(1)
    tile = SB_TILE
    n_chain = 2 * DF_HEADS

    @pl.when(i == 0)
    def _():
        zb_ref[...] = _logit_bounds((sqa_ref, sqb_ref), sk_ref, gsum_ref)
        lbound = _logit_bounds((dqa_ref, dqb_ref), dk_ref, gsum_ref)
        lane = lax.broadcasted_iota(jnp.int32, (1, LANES), 1)
        for c in range(n_chain):
            lb_ref[c] = jnp.max(jnp.where(lane == c, lbound, 0.0))
        worst = 2.0 * _lane_reduce(lbound, n_chain, jnp.max, -jnp.inf)
        plan_ref[0] = (worst <= MAX_FIXED_REF_LOG2).astype(jnp.int32)
        reach = (2.0 * _lane_reduce(lbound, 2, jnp.max, -jnp.inf) + SKIP_LOG2) / slope_ref[0]
        kept = jnp.int32(1)
        for t in range(1, dk_ref.shape[0] // TK):
            kept += (reach >= t).astype(jnp.int32)
        plan_ref[1] = kept

    fixed_ref_ok = plan_ref[0] == 1
    n_near = jnp.minimum(plan_ref[1], i)

    sq_half_refs = (sqa_ref, sqb_ref)

    def sb_block(j, carries, diag, r0=0, r1=SB_TILE):
        start = pl.multiple_of(j * tile, tile)
        q_rows = pl.ds(pl.multiple_of(i * tile, tile) + r0, r1 - r0)
        tri2 = tri2_ref[...]
        if diag:
            row = lax.broadcasted_iota(jnp.int32, (r1 - r0, tile), 0) + r0
            col = lax.broadcasted_iota(jnp.int32, (r1 - r0, tile), 1)
            strict = col < row
        new_carries, pvs = [], []
        for g in range(N_GROUPS):
            cols = slice(g * LANES, (g + 1) * LANES)
            kb = sk_ref[pl.ds(start, tile), cols]
            v_heads = _lane_halves(sv_ref[pl.ds(start, tile), cols])
            ws = []
            for sub in range(2):
                h = 2 * g + sub
                z = _nt_dot(sq_half_refs[sub][q_rows, cols], kb)
                sp = _softplus2(z)
                if diag:
                    sp = jnp.where(strict, sp, 0.0)
                hi = sp.astype(BF16)
                lo = (sp - hi.astype(F32)).astype(BF16)
                suffix = _dot(jnp.concatenate([hi, lo], axis=1), tri2)
                w = jnp.exp2(z - suffix - carries[h])
                if diag:
                    w = jnp.where(strict, w, 0.0)
                new_carries.append(carries[h] + jnp.sum(sp, axis=1, keepdims=True))
                ws.append(w.astype(BF16))
            pvs.append(_dot(jnp.concatenate(ws, axis=1), jnp.concatenate(v_heads, axis=0)))
        return tuple(new_carries), pvs

    def skip_slack(carries):
        least = _lane_pack([jnp.min(c, axis=0, keepdims=True) for c in carries])
        return _lane_reduce(least - zb_ref[...], SB_HEADS, jnp.min, jnp.inf)

    zeros = (jnp.zeros((tile, 1), F32),) * SB_HEADS

    def sb_diagonal_only():
        carries, total = sb_block(i, zeros, True)
        for g in range(N_GROUPS):
            osb_ref[:, g * LANES:(g + 1) * LANES] = total[g]
        return carries

    def sb_diagonal_and_previous():
        p = SB_PREV_ROWS
        carries, total = sb_block(i, zeros, True)
        top, pvs = sb_block(i - 1, tuple(c[:p] for c in carries), False, 0, p)
        for g in range(N_GROUPS):
            cols = slice(g * LANES, (g + 1) * LANES)
            osb_ref[:p, cols] = total[g][:p] + pvs[g]
            osb_ref[p:, cols] = total[g][p:]
        return top, tuple(c[p:] for c in carries)

    def sb_settle_rest(top, rest):
        p = SB_PREV_ROWS
        rest_slack = skip_slack(rest)

        def previous_for_rest(_):
            new_rest, pvs = sb_block(i - 1, rest, False, p, tile)
            for g in range(N_GROUPS):
                osb_ref[p:, g * LANES:(g + 1) * LANES] += pvs[g]
            return (skip_slack(new_rest),) + new_rest

        fixed = lax.cond(rest_slack < SKIP_LOG2, previous_for_rest,
                         lambda _: (rest_slack,) + rest, 0)
        carries = tuple(jnp.concatenate([a, b], axis=0) for a, b in zip(top, fixed[1:]))
        return (jnp.int32(1), jnp.minimum(skip_slack(top), fixed[0])) + carries

    dq_rows = pl.ds(pl.multiple_of(i * TQ, TQ), TQ)
    dq_half_refs = (dqa_ref, dqb_ref)

    def scores(j, c, bias):
        hd = c // 2
        cols = slice(hd * LANES, (hd + 1) * LANES)
        start = pl.multiple_of(j * TK, TK)
        return _nt_dot(dq_half_refs[c % 2][dq_rows, cols], dk_ref[pl.ds(start, TK), cols]) + bias

    ones = jnp.ones((TK, LANES), BF16)

    def values(j, hd):
        vb = dv_ref[pl.ds(pl.multiple_of(j * TK, TK), TK), hd * LANES:(hd + 1) * LANES]
        return jnp.concatenate([vb, ones], axis=1)

    def block_shift(t, hd):
        return slope_ref[hd] * jnp.asarray(t + 1).astype(F32)

    def diagonal(c):
        return jnp.exp2(scores(i, c, dbias_ref[c // 2]) + lb_ref[c]).astype(BF16)

    def earlier(t, c):
        hd = c // 2
        offset = lb_ref[c] - block_shift(t, hd)
        return jnp.exp2(scores(i - 1 - t, c, bias_ref[hd]) + offset).astype(BF16)

    def contract(ps, js, hd):
        return _dot(jnp.concatenate(ps, axis=1), jnp.concatenate([values(j, hd) for j in js], axis=0))

    def df_start_with(n_merged):
        for c in range(n_chain):
            ps = [diagonal(c)] + [earlier(t, c) for t in range(n_merged)]
            acc_ref[c] = contract(ps, [i - t for t in range(n_merged + 1)], c // 2)

    def df_remaining_blocks():
        def add_blocks(ts, chains):
            for c in chains:
                ps = [earlier(t, c) for t in ts]
                acc_ref[c] += contract(ps, [i - 1 - t for t in ts], c // 2)

        first_far = jnp.maximum(n_near, 2)

        def near(t, carry):
            add_blocks((t,), range(n_chain))
            return carry

        lax.fori_loop(2, first_far, near, 0)

        far_chains = range(2, n_chain)
        n_far = i - first_far

        def far_pair(u, carry):
            add_blocks((first_far + 2 * u, first_far + 2 * u + 1), far_chains)
            return carry

        lax.fori_loop(0, lax.shift_right_logical(n_far, 1), far_pair, 0)

        @pl.when(lax.bitwise_and(n_far, 1) == 1)
        def _():
            add_blocks((i - 1,), far_chains)

    def df_running_max():
        ms = []
        for c in range(n_chain):
            s = scores(i, c, dbias_ref[c // 2])
            m = jnp.max(s, axis=1, keepdims=True)
            ms.append(m)
            acc_ref[c] = contract([jnp.exp2(s - m).astype(BF16)], [i], c // 2)

        def body(t, m_run):
            j = i - 1 - t
            m_out = []
            for c in range(n_chain):
                hd = c // 2
                shift = block_shift(t, hd)
                s = scores(j, c, bias_ref[hd])
                m_new = jnp.maximum(m_run[c], jnp.max(s, axis=1, keepdims=True) - shift)
                alpha = jnp.exp2(m_run[c] - m_new)
                p = jnp.exp2(s - (m_new + shift))
                m_out.append(m_new)
                acc_ref[c] = alpha * acc_ref[c] + _dot(p.astype(BF16), values(j, hd))
            return tuple(m_out)

        lax.fori_loop(0, i, body, tuple(ms))

    def first_step(_):
        carries = sb_diagonal_only()
        df_start_with(0)
        return (jnp.int32(0), skip_slack(carries)) + carries

    def second_step(_):
        top, rest = sb_diagonal_and_previous()
        df_start_with(1)
        return sb_settle_rest(top, rest)

    def later_step(_):
        top, rest = sb_diagonal_and_previous()
        df_start_with(2)
        state = sb_settle_rest(top, rest)
        df_remaining_blocks()
        return state

    def constant_reference(_):
        return lax.cond(i == 0, first_step,
                        lambda _: lax.cond(i == 1, second_step, later_step, 0), 0)

    def running_reference(_):
        def with_previous(_):
            return sb_settle_rest(*sb_diagonal_and_previous())

        def without(_):
            carries = sb_diagonal_only()
            return (jnp.int32(0), skip_slack(carries)) + carries

        state = lax.cond(i > 0, with_previous, without, 0)
        df_running_max()
        return state

    state = lax.cond(fixed_ref_ok, constant_reference, running_reference, 0)

    def cond(state):
        return jnp.logical_and(state[0] < i, state[1] < SKIP_LOG2)

    def body(state):
        t = state[0]
        carries, pvs = sb_block(i - 1 - t, state[2:], False)
        for g in range(N_GROUPS):
            osb_ref[:, g * LANES:(g + 1) * LANES] += pvs[g]
        return (t + 1, skip_slack(carries)) + carries

    lax.while_loop(cond, body, state)

    dots = jnp.sum(lq_ref[...] * lk_ref[...], axis=1, keepdims=True)
    lam = jnp.exp(dots[0:1]) - jnp.exp(dots[1:2]) + lambda_init
    for hd in range(DF_HEADS):
        a0, a1 = acc_ref[2 * hd], acc_ref[2 * hd + 1]
        odf_ref[:, hd * LANES:(hd + 1) * LANES] = (
            a0[:, :LANES] / a0[:, LANES:] - lam * (a1[:, :LANES] / a1[:, LANES:]))


def _attention(proj, tri2, slopes, bias, dbias, lq, lk, gsum, batch, seq, lambda_init):
    nq = seq // TQ
    t = proj.shape[0]
    const3 = lambda b, i: (0, 0, 0)
    const2 = lambda b, i: (0, 0)
    sequence = lambda slab: pl.BlockSpec((seq, SLAB), lambda b, i: (b, slab))
    out = pl.BlockSpec((TQ, SLAB), lambda b, i: (b * nq + i, 0))
    return pl.pallas_call(
        functools.partial(_attention_kernel, lambda_init=lambda_init),
        grid=(batch, nq),
        in_specs=[
            pl.BlockSpec(memory_space=pltpu.SMEM),
            sequence(SB_Q_HALVES[0]), sequence(SB_Q_HALVES[1]), sequence(1), sequence(2),
            pl.BlockSpec((2 * SB_TILE, SB_TILE), const2),
            sequence(DF_Q_HALVES[0]), sequence(DF_Q_HALVES[1]), sequence(5), sequence(6),
            pl.BlockSpec((DF_HEADS, TQ, TK), const3),
            pl.BlockSpec((DF_HEADS, TQ, TK), const3),
            pl.BlockSpec((2, HEAD_DIM), const2),
            pl.BlockSpec((2, HEAD_DIM), const2),
            pl.BlockSpec((SLAB, LANES), const2),
        ],
        out_specs=[out, out],
        out_shape=[jax.ShapeDtypeStruct((t, SLAB), F32), jax.ShapeDtypeStruct((t, SLAB), F32)],
        scratch_shapes=[pltpu.VMEM((1, LANES), F32),
                        pltpu.VMEM((2 * DF_HEADS, TQ, 2 * LANES), F32),
                        pltpu.SMEM((2,), jnp.int32),
                        pltpu.SMEM((2 * DF_HEADS,), F32)],
        compiler_params=pltpu.CompilerParams(
            dimension_semantics=("arbitrary", "arbitrary"), vmem_limit_bytes=VMEM_LIMIT),
        name="attention",
    )(slopes, proj, proj, proj, proj, tri2, proj, proj, proj, proj, bias, dbias, lq, lk, gsum)


def _out_kernel(x_ref, osb_ref, odf_ref, zsb_ref, zdf_ref, gpre_ref, gpost_ref, ghead_ref,
                wg_ref, bg_ref, wsb_ref, wdf_ref, wo_ref, o_ref, m_ref,
                *, col_chunk, row_block, lambda_init):
    d = x_ref.shape[1]
    for r in range(0, x_ref.shape[0], row_block):
        rows = slice(r, r + row_block)
        x = x_ref[rows, :]
        h = _rms(x, gpre_ref[...]).astype(BF16)
        a_sb = (osb_ref[rows, :] * _silu(zsb_ref[rows, :].astype(F32))).astype(BF16)
        a_df = []
        for hd in range(DF_HEADS):
            cols = slice(hd * LANES, (hd + 1) * LANES)
            o = _rms(odf_ref[rows, cols], ghead_ref[...]) * (1.0 - lambda_init)
            a_df.append((o * _silu(zdf_ref[rows, cols].astype(F32))).astype(BF16))
        a_df = jnp.concatenate(a_df, axis=1)
        for n in range(0, d, col_chunk):
            cols = slice(n, n + col_chunk)
            dcols = slice(d + n, d + n + col_chunk)
            g_sb = jax.nn.sigmoid(_dot(h, wg_ref[:, cols]) + bg_ref[:, cols])
            g_df = jax.nn.sigmoid(_dot(h, wg_ref[:, dcols]) + bg_ref[:, dcols])
            y_sb = _dot(a_sb, wsb_ref[:, cols])
            y_df = _dot(a_df, wdf_ref[:, cols])
            m_ref[rows, cols] = (g_sb * y_sb + g_df * y_df).astype(BF16)
        out = _dot(m_ref[rows, :], wo_ref[...])
        o_ref[rows, :] = x + _rms(out, gpost_ref[...])


def _output_stage(xf, o_sb, o_df, proj, gpre, gpost, ghead, wg, bg, wsb, wdf, wo, lambda_init):
    t, d = xf.shape
    const = lambda i: (0, 0)
    resident = lambda a: pl.BlockSpec(a.shape, const, pipeline_mode=pl.Buffered(1))
    rows = lambda width, col: pl.BlockSpec((TM_OUT, width), lambda i: (i, col))
    return pl.pallas_call(
        functools.partial(_out_kernel, col_chunk=512, row_block=256, lambda_init=lambda_init),
        grid=(t // TM_OUT,),
        in_specs=[
            rows(d, 0),
            rows(SLAB, 0),
            rows(SLAB, 0),
            rows(SLAB, 3),
            rows(SLAB, 7),
            pl.BlockSpec((1, d), const),
            pl.BlockSpec((1, d), const),
            pl.BlockSpec((1, LANES), const),
            resident(wg), pl.BlockSpec(bg.shape, const), resident(wsb), resident(wdf), resident(wo),
        ],
        out_specs=rows(d, 0),
        out_shape=jax.ShapeDtypeStruct((t, d), F32),
        scratch_shapes=[pltpu.VMEM((TM_OUT, d), BF16)],
        compiler_params=pltpu.CompilerParams(
            dimension_semantics=("arbitrary",), vmem_limit_bytes=VMEM_LIMIT),
        name="output_stage",
    )(xf, o_sb, o_df, proj, proj, gpre, gpost, ghead, wg, bg, wsb, wdf, wo)


def _constant_tiles():
    sb = np.arange(SB_TILE)
    tri = (sb[:, None] >= sb[None, :]).astype(np.float32)
    tri2 = np.concatenate([tri, tri], axis=0)
    r = np.arange(TQ)[:, None]
    c = np.arange(TK)[None, :]
    gsum = (np.arange(SLAB)[:, None] // HEAD_DIM == np.arange(LANES)[None, :]).astype(np.float32)
    slopes = np.array([2.0 ** (-8.0 * (h + 1) / DF_HEADS) for h in range(DF_HEADS)],
                      dtype=np.float64) * LOG2E
    rel = (r - c).astype(np.float64)
    bias = -slopes[:, None, None] * rel[None]
    visible = (c // CHUNK) <= (r // CHUNK)
    dbias = np.where(visible[None], -slopes[:, None, None] * np.abs(rel)[None], NEG_BIG)
    block_slopes = slopes * TQ
    return (jnp.asarray(tri2, BF16), jnp.asarray(gsum, BF16), jnp.asarray(block_slopes, F32),
            jnp.asarray(bias, F32), jnp.asarray(dbias, F32))


def kernel(x, norm_pre, norm_post, w_in, w_gate, b_gate, w_branch_sb, w_branch_df, w_out,
           lambda_q, lambda_k, df_head_norm):
    b, s, d = x.shape
    depth = w_in.shape[0]
    assert s % TQ == 0 and TQ == TK and TQ % CHUNK == 0
    assert s % SB_TILE == 0
    assert (b * s) % TM_PROJ == 0 and (b * s) % TM_OUT == 0
    tri2, gsum, block_slopes, bias, dbias = _constant_tiles()
    xf = x.reshape(b * s, d)
    for l in range(depth):
        lambda_init = 0.8 - 0.6 * float(np.exp(-0.3 * l))
        proj = _in_projection(xf, norm_pre[l][None], w_in[l].astype(BF16))
        o_sb, o_df = _attention(proj, tri2, block_slopes, bias, dbias, lambda_q[l].astype(F32),
                                lambda_k[l].astype(F32), gsum, b, s, lambda_init)
        xf = _output_stage(xf, o_sb, o_df, proj, norm_pre[l][None], norm_post[l][None],
                           df_head_norm[l][None].astype(F32),
                           w_gate[l].astype(BF16), b_gate[l][None],
                           w_branch_sb[l].astype(BF16), w_branch_df[l].astype(BF16),
                           w_out[l].astype(BF16), lambda_init)
    return xf.reshape(b, s, d)
```

```python
import functools
import math

import numpy as np
import jax
import jax.numpy as jnp
from jax import lax
from jax.experimental import pallas as pl
from jax.experimental.pallas import tpu as pltpu

F32 = jnp.float32
BF16 = jnp.bfloat16

EPS = 1e-6
CHUNK = 64
HEAD_DIM = 64
LANES = 128
N_GROUPS = 4
SLAB = N_GROUPS * LANES
SB_HEADS = SLAB // HEAD_DIM
DF_HEADS = 4
LOG2E = math.log2(math.e)
Q_SCALE = HEAD_DIM ** -0.5 * LOG2E
Q_SLABS = (0, 4)
SB_Q_HALVES = (0, 8)
DF_Q_HALVES = (4, 9)

TQ = 256
TK = 256
SB_TILE = 256
SB_PREV_ROWS = 192
TM_PROJ = 1024
TM_OUT = 1024
NEG_BIG = -1e30
SKIP_LOG2 = 135.0
BOUND_MARGIN = 1.01
MAX_FIXED_REF_LOG2 = 100.0
SOFTPLUS_CLAMP_LOG2 = 100.0

VMEM_LIMIT = 56 * 1024 * 1024


def _nt_dot(a, b):
    return lax.dot_general(a, b, (((1,), (1,)), ((), ())), preferred_element_type=F32)


def _dot(a, b):
    return jnp.dot(a, b, preferred_element_type=F32)


def _rms(x, gain):
    return x * lax.rsqrt(jnp.mean(x * x, axis=-1, keepdims=True) + EPS) * gain


def _silu(z):
    return z * jax.nn.sigmoid(z)


def _lane_halves(x):
    lane = lax.broadcasted_iota(jnp.int32, (1, LANES), 1)
    first = lane < HEAD_DIM
    zero = jnp.zeros_like(x)
    return jnp.where(first, x, zero), jnp.where(first, zero, x)


def _max_sq_group_norms(refs, gsum_ref):
    best = jnp.zeros((1, LANES), F32)
    for r in range(0, refs[0].shape[0], TK):
        xf = sum(ref[r:r + TK, :] for ref in refs).astype(F32)
        n2 = _dot((xf * xf).astype(BF16), gsum_ref[...])
        best = jnp.maximum(best, jnp.max(n2, axis=0, keepdims=True))
    return best


def _logit_bounds(q_half_refs, k_ref, gsum_ref):
    qn = _max_sq_group_norms(q_half_refs, gsum_ref)
    kn = _max_sq_group_norms((k_ref,), gsum_ref)
    return jnp.sqrt(qn * kn) * BOUND_MARGIN


def _lane_pack(column_values):
    lane = lax.broadcasted_iota(jnp.int32, (1, LANES), 1)
    row = jnp.zeros((1, LANES), F32)
    for h, v in enumerate(column_values):
        row = jnp.where(lane == h, v, row)
    return row


def _lane_reduce(row, n, op, fill):
    lane = lax.broadcasted_iota(jnp.int32, (1, LANES), 1)
    return op(jnp.where(lane < n, row, fill))


def _proj_kernel(x_ref, g_ref, w_ref, o_ref, *, row_block):
    n_slabs = w_ref.shape[1] // SLAB
    lane = lax.broadcasted_iota(jnp.int32, (1, SLAB), 1)
    first = lax.bitwise_and(lane, LANES - 1) < HEAD_DIM
    for r in range(0, x_ref.shape[0], row_block):
        rows = slice(r, r + row_block)
        h = _rms(x_ref[rows, :], g_ref[...]).astype(BF16)
        for slab in range(n_slabs):
            cols = slice(slab * SLAB, (slab + 1) * SLAB)
            y = _dot(h, w_ref[:, cols])
            if slab in Q_SLABS:
                y = y * Q_SCALE
                extra = n_slabs + Q_SLABS.index(slab)
                o_ref[rows, extra * SLAB:(extra + 1) * SLAB] = jnp.where(first, 0.0, y).astype(BF16)
                y = jnp.where(first, y, 0.0)
            o_ref[rows, cols] = y.astype(BF16)


def _in_projection(xf, gain, w):
    t, d = xf.shape
    n = w.shape[1]
    n_out = n + len(Q_SLABS) * SLAB
    return pl.pallas_call(
        functools.partial(_proj_kernel, row_block=512),
        grid=(t // TM_PROJ,),
        in_specs=[
            pl.BlockSpec((TM_PROJ, d), lambda i: (i, 0)),
            pl.BlockSpec((1, d), lambda i: (0, 0)),
            pl.BlockSpec((d, n), lambda i: (0, 0), pipeline_mode=pl.Buffered(1)),
        ],
        out_specs=pl.BlockSpec((TM_PROJ, n_out), lambda i: (i, 0)),
        out_shape=jax.ShapeDtypeStruct((t, n_out), BF16),
        compiler_params=pltpu.CompilerParams(
            dimension_semantics=("arbitrary",), vmem_limit_bytes=VMEM_LIMIT),
        name="in_projection",
    )(xf, gain, w)


def _softplus2(z2):
    return jnp.maximum(z2, jnp.log2(1.0 + jnp.exp2(jnp.minimum(z2, SOFTPLUS_CLAMP_LOG2))))


def _sb_kernel(qa_ref, qb_ref, k_ref, v_ref, tri2_ref, gsum_ref, o_ref, zb_ref):
    i = pl.program_id(1)

    @pl.when(i == 0)
    def _():
        zb_ref[...] = _logit_bounds((qa_ref, qb_ref), k_ref, gsum_ref)

    tile = SB_TILE
    q_half_refs = (qa_ref, qb_ref)

    def block(j, carries, diag, r0=0, r1=SB_TILE):
        start = pl.multiple_of(j * tile, tile)
        q_rows = pl.ds(pl.multiple_of(i * tile, tile) + r0, r1 - r0)
        tri2 = tri2_ref[...]
        if diag:
            row = lax.broadcasted_iota(jnp.int32, (r1 - r0, tile), 0) + r0
            col = lax.broadcasted_iota(jnp.int32, (r1 - r0, tile), 1)
            strict = col < row
        new_carries, pvs = [], []
        for g in range(N_GROUPS):
            cols = slice(g * LANES, (g + 1) * LANES)
            kb = k_ref[pl.ds(start, tile), cols]
            v_heads = _lane_halves(v_ref[pl.ds(start, tile), cols])
            ws = []
            for sub in range(2):
                h = 2 * g + sub
                z = _nt_dot(q_half_refs[sub][q_rows, cols], kb)
                sp = _softplus2(z)
                if diag:
                    sp = jnp.where(strict, sp, 0.0)
                hi = sp.astype(BF16)
                lo = (sp - hi.astype(F32)).astype(BF16)
                suffix = _dot(jnp.concatenate([hi, lo], axis=1), tri2)
                w = jnp.exp2(z - suffix - carries[h])
                if diag:
                    w = jnp.where(strict, w, 0.0)
                new_carries.append(carries[h] + jnp.sum(sp, axis=1, keepdims=True))
                ws.append(w.astype(BF16))
            pvs.append(_dot(jnp.concatenate(ws, axis=1), jnp.concatenate(v_heads, axis=0)))
        return tuple(new_carries), pvs

    def skip_slack(carries):
        least = _lane_pack([jnp.min(c, axis=0, keepdims=True) for c in carries])
        return _lane_reduce(least - zb_ref[...], SB_HEADS, jnp.min, jnp.inf)

    zeros = (jnp.zeros((tile, 1), F32),) * SB_HEADS

    def diagonal_only(_):
        carries, total = block(i, zeros, True)
        for g in range(N_GROUPS):
            o_ref[:, g * LANES:(g + 1) * LANES] = total[g]
        return (jnp.int32(0), skip_slack(carries)) + carries

    def diagonal_and_previous(_):
        p = SB_PREV_ROWS
        carries, total = block(i, zeros, True)
        top, pvs = block(i - 1, tuple(c[:p] for c in carries), False, 0, p)
        for g in range(N_GROUPS):
            cols = slice(g * LANES, (g + 1) * LANES)
            o_ref[:p, cols] = total[g][:p] + pvs[g]
            o_ref[p:, cols] = total[g][p:]
        rest = tuple(c[p:] for c in carries)
        rest_slack = skip_slack(rest)

        def previous_for_rest(_):
            new_rest, pvs = block(i - 1, rest, False, p, tile)
            for g in range(N_GROUPS):
                o_ref[p:, g * LANES:(g + 1) * LANES] += pvs[g]
            return (skip_slack(new_rest),) + new_rest

        fixed = lax.cond(rest_slack < SKIP_LOG2, previous_for_rest,
                         lambda _: (rest_slack,) + rest, 0)
        carries = tuple(jnp.concatenate([a, b], axis=0) for a, b in zip(top, fixed[1:]))
        return (jnp.int32(1), jnp.minimum(skip_slack(top), fixed[0])) + carries

    state = lax.cond(i > 0, diagonal_and_previous, diagonal_only, 0)

    def cond(state):
        return jnp.logical_and(state[0] < i, state[1] < SKIP_LOG2)

    def body(state):
        t = state[0]
        carries, pvs = block(i - 1 - t, state[2:], False)
        for g in range(N_GROUPS):
            o_ref[:, g * LANES:(g + 1) * LANES] += pvs[g]
        return (t + 1, skip_slack(carries)) + carries

    lax.while_loop(cond, body, state)


def _sb_attention(proj, tri2, gsum, batch, seq):
    nq = seq // SB_TILE
    t = proj.shape[0]
    return pl.pallas_call(
        _sb_kernel,
        grid=(batch, nq),
        in_specs=[
            pl.BlockSpec((seq, SLAB), lambda b, i: (b, SB_Q_HALVES[0])),
            pl.BlockSpec((seq, SLAB), lambda b, i: (b, SB_Q_HALVES[1])),
            pl.BlockSpec((seq, SLAB), lambda b, i: (b, 1)),
            pl.BlockSpec((seq, SLAB), lambda b, i: (b, 2)),
            pl.BlockSpec((2 * SB_TILE, SB_TILE), lambda b, i: (0, 0)),
            pl.BlockSpec((SLAB, LANES), lambda b, i: (0, 0)),
        ],
        out_specs=pl.BlockSpec((SB_TILE, SLAB), lambda b, i: (b * nq + i, 0)),
        out_shape=jax.ShapeDtypeStruct((t, SLAB), F32),
        scratch_shapes=[pltpu.VMEM((1, LANES), F32)],
        compiler_params=pltpu.CompilerParams(
            dimension_semantics=("arbitrary", "arbitrary"), vmem_limit_bytes=VMEM_LIMIT),
        name="sb_attention",
    )(proj, proj, proj, proj, tri2, gsum)


def _df_kernel(slope_ref, qa_ref, qb_ref, k_ref, v_ref, bias_ref, dbias_ref, lq_ref, lk_ref,
               gsum_ref, o_ref, acc_ref, plan_ref, lb_ref, *, lambda_init):
    i = pl.program_id(1)
    n_chain = 2 * DF_HEADS

    @pl.when(i == 0)
    def _():
        lbound = _logit_bounds((qa_ref, qb_ref), k_ref, gsum_ref)
        lane = lax.broadcasted_iota(jnp.int32, (1, LANES), 1)
        for c in range(n_chain):
            lb_ref[c] = jnp.max(jnp.where(lane == c, lbound, 0.0))
        worst = 2.0 * _lane_reduce(lbound, n_chain, jnp.max, -jnp.inf)
        plan_ref[0] = (worst <= MAX_FIXED_REF_LOG2).astype(jnp.int32)
        reach = (2.0 * _lane_reduce(lbound, 2, jnp.max, -jnp.inf) + SKIP_LOG2) / slope_ref[0]
        kept = jnp.int32(1)
        for t in range(1, k_ref.shape[0] // TK):
            kept += (reach >= t).astype(jnp.int32)
        plan_ref[1] = kept

    fixed_ref_ok = plan_ref[0] == 1
    n_near = jnp.minimum(plan_ref[1], i)

    q_rows = pl.ds(pl.multiple_of(i * TQ, TQ), TQ)
    q_half_refs = (qa_ref, qb_ref)

    def scores(j, c, bias):
        hd = c // 2
        cols = slice(hd * LANES, (hd + 1) * LANES)
        start = pl.multiple_of(j * TK, TK)
        return _nt_dot(q_half_refs[c % 2][q_rows, cols], k_ref[pl.ds(start, TK), cols]) + bias

    ones = jnp.ones((TK, LANES), BF16)

    def values(j, hd):
        vb = v_ref[pl.ds(pl.multiple_of(j * TK, TK), TK), hd * LANES:(hd + 1) * LANES]
        return jnp.concatenate([vb, ones], axis=1)

    def block_shift(t, hd):
        return slope_ref[hd] * jnp.asarray(t + 1).astype(F32)

    def diagonal(c):
        return jnp.exp2(scores(i, c, dbias_ref[c // 2]) + lb_ref[c]).astype(BF16)

    def earlier(t, c):
        hd = c // 2
        offset = lb_ref[c] - block_shift(t, hd)
        return jnp.exp2(scores(i - 1 - t, c, bias_ref[hd]) + offset).astype(BF16)

    def contract(ps, js, hd):
        return _dot(jnp.concatenate(ps, axis=1), jnp.concatenate([values(j, hd) for j in js], axis=0))

    def start_with(n_merged):
        for c in range(n_chain):
            ps = [diagonal(c)] + [earlier(t, c) for t in range(n_merged)]
            acc_ref[c] = contract(ps, [i - t for t in range(n_merged + 1)], c // 2)

    @pl.when(jnp.logical_and(fixed_ref_ok, i == 0))
    def _():
        start_with(0)

    @pl.when(jnp.logical_and(fixed_ref_ok, i == 1))
    def _():
        start_with(1)

    @pl.when(jnp.logical_and(fixed_ref_ok, i >= 2))
    def _():
        start_with(2)

        def add_blocks(ts, chains):
            for c in chains:
                ps = [earlier(t, c) for t in ts]
                acc_ref[c] += contract(ps, [i - 1 - t for t in ts], c // 2)

        first_far = jnp.maximum(n_near, 2)

        def near(t, carry):
            add_blocks((t,), range(n_chain))
            return carry

        lax.fori_loop(2, first_far, near, 0)

        far_chains = range(2, n_chain)
        n_far = i - first_far

        def far_pair(u, carry):
            add_blocks((first_far + 2 * u, first_far + 2 * u + 1), far_chains)
            return carry

        lax.fori_loop(0, lax.shift_right_logical(n_far, 1), far_pair, 0)

        @pl.when(lax.bitwise_and(n_far, 1) == 1)
        def _():
            add_blocks((i - 1,), far_chains)

    @pl.when(jnp.logical_not(fixed_ref_ok))
    def _():
        ms = []
        for c in range(n_chain):
            s = scores(i, c, dbias_ref[c // 2])
            m = jnp.max(s, axis=1, keepdims=True)
            ms.append(m)
            acc_ref[c] = contract([jnp.exp2(s - m).astype(BF16)], [i], c // 2)

        def body(t, m_run):
            j = i - 1 - t
            m_out = []
            for c in range(n_chain):
                hd = c // 2
                shift = block_shift(t, hd)
                s = scores(j, c, bias_ref[hd])
                m_new = jnp.maximum(m_run[c], jnp.max(s, axis=1, keepdims=True) - shift)
                alpha = jnp.exp2(m_run[c] - m_new)
                p = jnp.exp2(s - (m_new + shift))
                m_out.append(m_new)
                acc_ref[c] = alpha * acc_ref[c] + _dot(p.astype(BF16), values(j, hd))
            return tuple(m_out)

        lax.fori_loop(0, i, body, tuple(ms))

    dots = jnp.sum(lq_ref[...] * lk_ref[...], axis=1, keepdims=True)
    lam = jnp.exp(dots[0:1]) - jnp.exp(dots[1:2]) + lambda_init
    for hd in range(DF_HEADS):
        a0, a1 = acc_ref[2 * hd], acc_ref[2 * hd + 1]
        o_ref[:, hd * LANES:(hd + 1) * LANES] = (
            a0[:, :LANES] / a0[:, LANES:] - lam * (a1[:, :LANES] / a1[:, LANES:]))


def _df_attention(proj, slopes, bias, dbias, lq, lk, gsum, batch, seq, lambda_init):
    nq = seq // TQ
    t = proj.shape[0]
    const3 = lambda b, i: (0, 0, 0)
    const2 = lambda b, i: (0, 0)
    return pl.pallas_call(
        functools.partial(_df_kernel, lambda_init=lambda_init),
        grid=(batch, nq),
        in_specs=[
            pl.BlockSpec(memory_space=pltpu.SMEM),
            pl.BlockSpec((seq, SLAB), lambda b, i: (b, DF_Q_HALVES[0])),
            pl.BlockSpec((seq, SLAB), lambda b, i: (b, DF_Q_HALVES[1])),
            pl.BlockSpec((seq, SLAB), lambda b, i: (b, 5)),
            pl.BlockSpec((seq, SLAB), lambda b, i: (b, 6)),
            pl.BlockSpec((DF_HEADS, TQ, TK), const3),
            pl.BlockSpec((DF_HEADS, TQ, TK), const3),
            pl.BlockSpec((2, HEAD_DIM), const2),
            pl.BlockSpec((2, HEAD_DIM), const2),
            pl.BlockSpec((SLAB, LANES), const2),
        ],
        out_specs=pl.BlockSpec((TQ, SLAB), lambda b, i: (b * nq + i, 0)),
        out_shape=jax.ShapeDtypeStruct((t, SLAB), F32),
        scratch_shapes=[pltpu.VMEM((2 * DF_HEADS, TQ, 2 * LANES), F32),
                        pltpu.SMEM((2,), jnp.int32),
                        pltpu.SMEM((2 * DF_HEADS,), F32)],
        compiler_params=pltpu.CompilerParams(
            dimension_semantics=("arbitrary", "arbitrary"), vmem_limit_bytes=VMEM_LIMIT),
        name="df_attention",
    )(slopes, proj, proj, proj, proj, bias, dbias, lq, lk, gsum)


def _out_kernel(x_ref, osb_ref, odf_ref, zsb_ref, zdf_ref, gpre_ref, gpost_ref, ghead_ref,
                wg_ref, bg_ref, wsb_ref, wdf_ref, wo_ref, o_ref, m_ref,
                *, col_chunk, row_block, lambda_init):
    d = x_ref.shape[1]
    for r in range(0, x_ref.shape[0], row_block):
        rows = slice(r, r + row_block)
        x = x_ref[rows, :]
        h = _rms(x, gpre_ref[...]).astype(BF16)
        a_sb = (osb_ref[rows, :] * _silu(zsb_ref[rows, :].astype(F32))).astype(BF16)
        a_df = []
        for hd in range(DF_HEADS):
            cols = slice(hd * LANES, (hd + 1) * LANES)
            o = _rms(odf_ref[rows, cols], ghead_ref[...]) * (1.0 - lambda_init)
            a_df.append((o * _silu(zdf_ref[rows, cols].astype(F32))).astype(BF16))
        a_df = jnp.concatenate(a_df, axis=1)
        for n in range(0, d, col_chunk):
            cols = slice(n, n + col_chunk)
            dcols = slice(d + n, d + n + col_chunk)
            g_sb = jax.nn.sigmoid(_dot(h, wg_ref[:, cols]) + bg_ref[:, cols])
            g_df = jax.nn.sigmoid(_dot(h, wg_ref[:, dcols]) + bg_ref[:, dcols])
            y_sb = _dot(a_sb, wsb_ref[:, cols])
            y_df = _dot(a_df, wdf_ref[:, cols])
            m_ref[rows, cols] = (g_sb * y_sb + g_df * y_df).astype(BF16)
        out = _dot(m_ref[rows, :], wo_ref[...])
        o_ref[rows, :] = x + _rms(out, gpost_ref[...])


def _output_stage(xf, o_sb, o_df, proj, gpre, gpost, ghead, wg, bg, wsb, wdf, wo, lambda_init):
    t, d = xf.shape
    const = lambda i: (0, 0)
    resident = lambda a: pl.BlockSpec(a.shape, const, pipeline_mode=pl.Buffered(1))
    rows = lambda width, col: pl.BlockSpec((TM_OUT, width), lambda i: (i, col))
    return pl.pallas_call(
        functools.partial(_out_kernel, col_chunk=512, row_block=256, lambda_init=lambda_init),
        grid=(t // TM_OUT,),
        in_specs=[
            rows(d, 0),
            rows(SLAB, 0),
            rows(SLAB, 0),
            rows(SLAB, 3),
            rows(SLAB, 7),
            pl.BlockSpec((1, d), const),
            pl.BlockSpec((1, d), const),
            pl.BlockSpec((1, LANES), const),
            resident(wg), pl.BlockSpec(bg.shape, const), resident(wsb), resident(wdf), resident(wo),
        ],
        out_specs=rows(d, 0),
        out_shape=jax.ShapeDtypeStruct((t, d), F32),
        scratch_shapes=[pltpu.VMEM((TM_OUT, d), BF16)],
        compiler_params=pltpu.CompilerParams(
            dimension_semantics=("arbitrary",), vmem_limit_bytes=VMEM_LIMIT),
        name="output_stage",
    )(xf, o_sb, o_df, proj, proj, gpre, gpost, ghead, wg, bg, wsb, wdf, wo)


def _constant_tiles():
    sb = np.arange(SB_TILE)
    tri = (sb[:, None] >= sb[None, :]).astype(np.float32)
    tri2 = np.concatenate([tri, tri], axis=0)
    r = np.arange(TQ)[:, None]
    c = np.arange(TK)[None, :]
    gsum = (np.arange(SLAB)[:, None] // HEAD_DIM == np.arange(LANES)[None, :]).astype(np.float32)
    slopes = np.array([2.0 ** (-8.0 * (h + 1) / DF_HEADS) for h in range(DF_HEADS)],
                      dtype=np.float64) * LOG2E
    rel = (r - c).astype(np.float64)
    bias = -slopes[:, None, None] * rel[None]
    visible = (c // CHUNK) <= (r // CHUNK)
    dbias = np.where(visible[None], -slopes[:, None, None] * np.abs(rel)[None], NEG_BIG)
    block_slopes = slopes * TQ
    return (jnp.asarray(tri2, BF16), jnp.asarray(gsum, BF16), jnp.asarray(block_slopes, F32),
            jnp.asarray(bias, F32), jnp.asarray(dbias, F32))


def kernel(x, norm_pre, norm_post, w_in, w_gate, b_gate, w_branch_sb, w_branch_df, w_out,
           lambda_q, lambda_k, df_head_norm):
    b, s, d = x.shape
    depth = w_in.shape[0]
    assert s % TQ == 0 and TQ == TK and TQ % CHUNK == 0
    assert s % SB_TILE == 0
    assert (b * s) % TM_PROJ == 0 and (b * s) % TM_OUT == 0
    tri2, gsum, block_slopes, bias, dbias = _constant_tiles()
    xf = x.reshape(b * s, d)
    for l in range(depth):
        lambda_init = 0.8 - 0.6 * float(np.exp(-0.3 * l))
        proj = _in_projection(xf, norm_pre[l][None], w_in[l].astype(BF16))
        o_sb = _sb_attention(proj, tri2, gsum, b, s)
        o_df = _df_attention(proj, block_slopes, bias, dbias, lambda_q[l].astype(F32),
                             lambda_k[l].astype(F32), gsum, b, s, lambda_init)
        xf = _output_stage(xf, o_sb, o_df, proj, norm_pre[l][None], norm_post[l][None],
                           df_head_norm[l][None].astype(F32),
                           w_gate[l].astype(BF16), b_gate[l][None],
                           w_branch_sb[l].astype(BF16), w_branch_df[l].astype(BF16),
                           w_out[l].astype(BF16), lambda_init)
    return xf.reshape(b, s, d)
```

```python
import functools
import math

import numpy as np
import jax
import jax.numpy as jnp
from jax import lax
from jax.experimental import pallas as pl
from jax.experimental.pallas import tpu as pltpu

F32 = jnp.float32
BF16 = jnp.bfloat16

EPS = 1e-6
CHUNK = 64
HEAD_DIM = 64
LANES = 128
N_GROUPS = 4
SLAB = N_GROUPS * LANES
SB_HEADS = SLAB // HEAD_DIM
DF_HEADS = 4
LOG2E = math.log2(math.e)
Q_SCALE = HEAD_DIM ** -0.5 * LOG2E
Q_SLABS = (0, 4)
SB_Q_HALVES = (0, 8)
DF_Q_HALVES = (4, 9)

TQ = 256
TK = 256
SB_TILE = 256
SB_PREV_ROWS = 192
TM_PROJ = 1024
TM_OUT = 1024
NEG_BIG = -1e30
SKIP_LOG2 = 135.0
BOUND_MARGIN = 1.01
MAX_FIXED_REF_LOG2 = 100.0
SOFTPLUS_CLAMP_LOG2 = 100.0

VMEM_LIMIT = 56 * 1024 * 1024


def _nt_dot(a, b):
    return lax.dot_general(a, b, (((1,), (1,)), ((), ())), preferred_element_type=F32)


def _dot(a, b):
    return jnp.dot(a, b, preferred_element_type=F32)


def _rms(x, gain):
    return x * lax.rsqrt(jnp.mean(x * x, axis=-1, keepdims=True) + EPS) * gain


def _silu(z):
    return z * jax.nn.sigmoid(z)


def _lane_halves(x):
    lane = lax.broadcasted_iota(jnp.int32, (1, LANES), 1)
    first = lane < HEAD_DIM
    zero = jnp.zeros_like(x)
    return jnp.where(first, x, zero), jnp.where(first, zero, x)


def _max_sq_group_norms(refs, gsum_ref):
    best = jnp.zeros((1, LANES), F32)
    for r in range(0, refs[0].shape[0], TK):
        xf = sum(ref[r:r + TK, :] for ref in refs).astype(F32)
        n2 = _dot((xf * xf).astype(BF16), gsum_ref[...])
        best = jnp.maximum(best, jnp.max(n2, axis=0, keepdims=True))
    return best


def _logit_bounds(q_half_refs, k_ref, gsum_ref):
    qn = _max_sq_group_norms(q_half_refs, gsum_ref)
    kn = _max_sq_group_norms((k_ref,), gsum_ref)
    return jnp.sqrt(qn * kn) * BOUND_MARGIN


def _lane_pack(column_values):
    lane = lax.broadcasted_iota(jnp.int32, (1, LANES), 1)
    row = jnp.zeros((1, LANES), F32)
    for h, v in enumerate(column_values):
        row = jnp.where(lane == h, v, row)
    return row


def _lane_reduce(row, n, op, fill):
    lane = lax.broadcasted_iota(jnp.int32, (1, LANES), 1)
    return op(jnp.where(lane < n, row, fill))


def _proj_kernel(x_ref, g_ref, w_ref, *rest, row_block):
    n_side = (len(rest) - 1) // 2
    o_ref = rest[n_side]
    for src_ref, dst_ref in zip(rest[:n_side], rest[n_side + 1:]):
        dst_ref[...] = src_ref[...].astype(BF16)
    n_slabs = w_ref.shape[1] // SLAB
    lane = lax.broadcasted_iota(jnp.int32, (1, SLAB), 1)
    first = lax.bitwise_and(lane, LANES - 1) < HEAD_DIM
    for r in range(0, x_ref.shape[0], row_block):
        rows = slice(r, r + row_block)
        h = _rms(x_ref[rows, :], g_ref[...]).astype(BF16)
        for slab in range(n_slabs):
            cols = slice(slab * SLAB, (slab + 1) * SLAB)
            y = _dot(h, w_ref[:, cols])
            if slab in Q_SLABS:
                y = y * Q_SCALE
                extra = n_slabs + Q_SLABS.index(slab)
                o_ref[rows, extra * SLAB:(extra + 1) * SLAB] = jnp.where(first, 0.0, y).astype(BF16)
                y = jnp.where(first, y, 0.0)
            o_ref[rows, cols] = y.astype(BF16)


def _in_projection(xf, gain, w, later_weights):
    t, d = xf.shape
    n = w.shape[1]
    n_out = n + len(Q_SLABS) * SLAB
    steps = t // TM_PROJ
    bands = []
    for a in later_weights:
        assert a.shape[0] % (steps * 16) == 0
        bands.append(pl.BlockSpec((a.shape[0] // steps, a.shape[1]), lambda i: (i, 0)))
    outs = pl.pallas_call(
        functools.partial(_proj_kernel, row_block=512),
        grid=(steps,),
        in_specs=[
            pl.BlockSpec((TM_PROJ, d), lambda i: (i, 0)),
            pl.BlockSpec((1, d), lambda i: (0, 0)),
            pl.BlockSpec((d, n), lambda i: (0, 0), pipeline_mode=pl.Buffered(1)),
        ] + bands,
        out_specs=[pl.BlockSpec((TM_PROJ, n_out), lambda i: (i, 0))] + bands,
        out_shape=[jax.ShapeDtypeStruct((t, n_out), BF16)]
        + [jax.ShapeDtypeStruct(a.shape, BF16) for a in later_weights],
        compiler_params=pltpu.CompilerParams(
            dimension_semantics=("arbitrary",), vmem_limit_bytes=VMEM_LIMIT),
        name="in_projection",
    )(xf, gain, w, *later_weights)
    return outs[0], outs[1:]


def _softplus2(z2):
    return jnp.maximum(z2, jnp.log2(1.0 + jnp.exp2(jnp.minimum(z2, SOFTPLUS_CLAMP_LOG2))))


def _sb_kernel(qa_ref, qb_ref, k_ref, v_ref, tri2_ref, gsum_ref, o_ref, zb_ref):
    i = pl.program_id(1)

    @pl.when(i == 0)
    def _():
        zb_ref[...] = _logit_bounds((qa_ref, qb_ref), k_ref, gsum_ref)

    tile = SB_TILE
    q_half_refs = (qa_ref, qb_ref)

    def block(j, carries, diag, r0=0, r1=SB_TILE):
        start = pl.multiple_of(j * tile, tile)
        q_rows = pl.ds(pl.multiple_of(i * tile, tile) + r0, r1 - r0)
        tri2 = tri2_ref[...]
        if diag:
            row = lax.broadcasted_iota(jnp.int32, (r1 - r0, tile), 0) + r0
            col = lax.broadcasted_iota(jnp.int32, (r1 - r0, tile), 1)
            strict = col < row
        new_carries, pvs = [], []
        for g in range(N_GROUPS):
            cols = slice(g * LANES, (g + 1) * LANES)
            kb = k_ref[pl.ds(start, tile), cols]
            v_heads = _lane_halves(v_ref[pl.ds(start, tile), cols])
            ws = []
            for sub in range(2):
                h = 2 * g + sub
                z = _nt_dot(q_half_refs[sub][q_rows, cols], kb)
                sp = _softplus2(z)
                if diag:
                    sp = jnp.where(strict, sp, 0.0)
                hi = sp.astype(BF16)
                lo = (sp - hi.astype(F32)).astype(BF16)
                suffix = _dot(jnp.concatenate([hi, lo], axis=1), tri2)
                w = jnp.exp2(z - suffix - carries[h])
                if diag:
                    w = jnp.where(strict, w, 0.0)
                new_carries.append(carries[h] + jnp.sum(sp, axis=1, keepdims=True))
                ws.append(w.astype(BF16))
            pvs.append(_dot(jnp.concatenate(ws, axis=1), jnp.concatenate(v_heads, axis=0)))
        return tuple(new_carries), pvs

    def skip_slack(carries):
        least = _lane_pack([jnp.min(c, axis=0, keepdims=True) for c in carries])
        return _lane_reduce(least - zb_ref[...], SB_HEADS, jnp.min, jnp.inf)

    zeros = (jnp.zeros((tile, 1), F32),) * SB_HEADS

    def diagonal_only(_):
        carries, total = block(i, zeros, True)
        for g in range(N_GROUPS):
            o_ref[:, g * LANES:(g + 1) * LANES] = total[g]
        return (jnp.int32(0), skip_slack(carries)) + carries

    def diagonal_and_previous(_):
        p = SB_PREV_ROWS
        carries, total = block(i, zeros, True)
        top, pvs = block(i - 1, tuple(c[:p] for c in carries), False, 0, p)
        for g in range(N_GROUPS):
            cols = slice(g * LANES, (g + 1) * LANES)
            o_ref[:p, cols] = total[g][:p] + pvs[g]
            o_ref[p:, cols] = total[g][p:]
        rest = tuple(c[p:] for c in carries)
        rest_slack = skip_slack(rest)

        def previous_for_rest(_):
            new_rest, pvs = block(i - 1, rest, False, p, tile)
            for g in range(N_GROUPS):
                o_ref[p:, g * LANES:(g + 1) * LANES] += pvs[g]
            return (skip_slack(new_rest),) + new_rest

        fixed = lax.cond(rest_slack < SKIP_LOG2, previous_for_rest,
                         lambda _: (rest_slack,) + rest, 0)
        carries = tuple(jnp.concatenate([a, b], axis=0) for a, b in zip(top, fixed[1:]))
        return (jnp.int32(1), jnp.minimum(skip_slack(top), fixed[0])) + carries

    state = lax.cond(i > 0, diagonal_and_previous, diagonal_only, 0)

    def cond(state):
        return jnp.logical_and(state[0] < i, state[1] < SKIP_LOG2)

    def body(state):
        t = state[0]
        carries, pvs = block(i - 1 - t, state[2:], False)
        for g in range(N_GROUPS):
            o_ref[:, g * LANES:(g + 1) * LANES] += pvs[g]
        return (t + 1, skip_slack(carries)) + carries

    lax.while_loop(cond, body, state)


def _sb_attention(proj, tri2, gsum, batch, seq):
    nq = seq // SB_TILE
    t = proj.shape[0]
    return pl.pallas_call(
        _sb_kernel,
        grid=(batch, nq),
        in_specs=[
            pl.BlockSpec((seq, SLAB), lambda b, i: (b, SB_Q_HALVES[0])),
            pl.BlockSpec((seq, SLAB), lambda b, i: (b, SB_Q_HALVES[1])),
            pl.BlockSpec((seq, SLAB), lambda b, i: (b, 1)),
            pl.BlockSpec((seq, SLAB), lambda b, i: (b, 2)),
            pl.BlockSpec((2 * SB_TILE, SB_TILE), lambda b, i: (0, 0)),
            pl.BlockSpec((SLAB, LANES), lambda b, i: (0, 0)),
        ],
        out_specs=pl.BlockSpec((SB_TILE, SLAB), lambda b, i: (b * nq + i, 0)),
        out_shape=jax.ShapeDtypeStruct((t, SLAB), F32),
        scratch_shapes=[pltpu.VMEM((1, LANES), F32)],
        compiler_params=pltpu.CompilerParams(
            dimension_semantics=("arbitrary", "arbitrary"), vmem_limit_bytes=VMEM_LIMIT),
        name="sb_attention",
    )(proj, proj, proj, proj, tri2, gsum)


def _df_kernel(slope_ref, qa_ref, qb_ref, k_ref, v_ref, bias_ref, dbias_ref, lq_ref, lk_ref,
               gsum_ref, o_ref, acc_ref, plan_ref, lb_ref, *, lambda_init):
    i = pl.program_id(1)
    n_chain = 2 * DF_HEADS

    @pl.when(i == 0)
    def _():
        lbound = _logit_bounds((qa_ref, qb_ref), k_ref, gsum_ref)
        lane = lax.broadcasted_iota(jnp.int32, (1, LANES), 1)
        for c in range(n_chain):
            lb_ref[c] = jnp.max(jnp.where(lane == c, lbound, 0.0))
        worst = 2.0 * _lane_reduce(lbound, n_chain, jnp.max, -jnp.inf)
        plan_ref[0] = (worst <= MAX_FIXED_REF_LOG2).astype(jnp.int32)
        reach = (2.0 * _lane_reduce(lbound, 2, jnp.max, -jnp.inf) + SKIP_LOG2) / slope_ref[0]
        kept = jnp.int32(1)
        for t in range(1, k_ref.shape[0] // TK):
            kept += (reach >= t).astype(jnp.int32)
        plan_ref[1] = kept

    fixed_ref_ok = plan_ref[0] == 1
    n_near = jnp.minimum(plan_ref[1], i)

    q_rows = pl.ds(pl.multiple_of(i * TQ, TQ), TQ)
    q_half_refs = (qa_ref, qb_ref)

    def scores(j, c, bias):
        hd = c // 2
        cols = slice(hd * LANES, (hd + 1) * LANES)
        start = pl.multiple_of(j * TK, TK)
        return _nt_dot(q_half_refs[c % 2][q_rows, cols], k_ref[pl.ds(start, TK), cols]) + bias

    ones = jnp.ones((TK, LANES), BF16)

    def values(j, hd):
        vb = v_ref[pl.ds(pl.multiple_of(j * TK, TK), TK), hd * LANES:(hd + 1) * LANES]
        return jnp.concatenate([vb, ones], axis=1)

    def block_shift(t, hd):
        return slope_ref[hd] * jnp.asarray(t + 1).astype(F32)

    def diagonal(c):
        return jnp.exp2(scores(i, c, dbias_ref[c // 2]) + lb_ref[c]).astype(BF16)

    def earlier(t, c):
        hd = c // 2
        offset = lb_ref[c] - block_shift(t, hd)
        return jnp.exp2(scores(i - 1 - t, c, bias_ref[hd]) + offset).astype(BF16)

    def contract(ps, js, hd):
        return _dot(jnp.concatenate(ps, axis=1), jnp.concatenate([values(j, hd) for j in js], axis=0))

    def start_with(n_merged):
        for c in range(n_chain):
            ps = [diagonal(c)] + [earlier(t, c) for t in range(n_merged)]
            acc_ref[c] = contract(ps, [i - t for t in range(n_merged + 1)], c // 2)

    @pl.when(jnp.logical_and(fixed_ref_ok, i == 0))
    def _():
        start_with(0)

    @pl.when(jnp.logical_and(fixed_ref_ok, i == 1))
    def _():
        start_with(1)

    @pl.when(jnp.logical_and(fixed_ref_ok, i >= 2))
    def _():
        start_with(2)

        def add_blocks(ts, chains):
            for c in chains:
                ps = [earlier(t, c) for t in ts]
                acc_ref[c] += contract(ps, [i - 1 - t for t in ts], c // 2)

        first_far = jnp.maximum(n_near, 2)

        def near(t, carry):
            add_blocks((t,), range(n_chain))
            return carry

        lax.fori_loop(2, first_far, near, 0)

        far_chains = range(2, n_chain)
        n_far = i - first_far

        def far_pair(u, carry):
            add_blocks((first_far + 2 * u, first_far + 2 * u + 1), far_chains)
            return carry

        lax.fori_loop(0, lax.shift_right_logical(n_far, 1), far_pair, 0)

        @pl.when(lax.bitwise_and(n_far, 1) == 1)
        def _():
            add_blocks((i - 1,), far_chains)

    @pl.when(jnp.logical_not(fixed_ref_ok))
    def _():
        ms = []
        for c in range(n_chain):
            s = scores(i, c, dbias_ref[c // 2])
            m = jnp.max(s, axis=1, keepdims=True)
            ms.append(m)
            acc_ref[c] = contract([jnp.exp2(s - m).astype(BF16)], [i], c // 2)

        def body(t, m_run):
            j = i - 1 - t
            m_out = []
            for c in range(n_chain):
                hd = c // 2
                shift = block_shift(t, hd)
                s = scores(j, c, bias_ref[hd])
                m_new = jnp.maximum(m_run[c], jnp.max(s, axis=1, keepdims=True) - shift)
                alpha = jnp.exp2(m_run[c] - m_new)
                p = jnp.exp2(s - (m_new + shift))
                m_out.append(m_new)
                acc_ref[c] = alpha * acc_ref[c] + _dot(p.astype(BF16), values(j, hd))
            return tuple(m_out)

        lax.fori_loop(0, i, body, tuple(ms))

    dots = jnp.sum(lq_ref[...] * lk_ref[...], axis=1, keepdims=True)
    lam = jnp.exp(dots[0:1]) - jnp.exp(dots[1:2]) + lambda_init
    for hd in range(DF_HEADS):
        a0, a1 = acc_ref[2 * hd], acc_ref[2 * hd + 1]
        o_ref[:, hd * LANES:(hd + 1) * LANES] = (
            a0[:, :LANES] / a0[:, LANES:] - lam * (a1[:, :LANES] / a1[:, LANES:]))


def _df_attention(proj, slopes, bias, dbias, lq, lk, gsum, batch, seq, lambda_init):
    nq = seq // TQ
    t = proj.shape[0]
    const3 = lambda b, i: (0, 0, 0)
    const2 = lambda b, i: (0, 0)
    return pl.pallas_call(
        functools.partial(_df_kernel, lambda_init=lambda_init),
        grid=(batch, nq),
        in_specs=[
            pl.BlockSpec(memory_space=pltpu.SMEM),
            pl.BlockSpec((seq, SLAB), lambda b, i: (b, DF_Q_HALVES[0])),
            pl.BlockSpec((seq, SLAB), lambda b, i: (b, DF_Q_HALVES[1])),
            pl.BlockSpec((seq, SLAB), lambda b, i: (b, 5)),
            pl.BlockSpec((seq, SLAB), lambda b, i: (b, 6)),
            pl.BlockSpec((DF_HEADS, TQ, TK), const3),
            pl.BlockSpec((DF_HEADS, TQ, TK), const3),
            pl.BlockSpec((2, HEAD_DIM), const2),
            pl.BlockSpec((2, HEAD_DIM), const2),
            pl.BlockSpec((SLAB, LANES), const2),
        ],
        out_specs=pl.BlockSpec((TQ, SLAB), lambda b, i: (b * nq + i, 0)),
        out_shape=jax.ShapeDtypeStruct((t, SLAB), F32),
        scratch_shapes=[pltpu.VMEM((2 * DF_HEADS, TQ, 2 * LANES), F32),
                        pltpu.SMEM((2,), jnp.int32),
                        pltpu.SMEM((2 * DF_HEADS,), F32)],
        compiler_params=pltpu.CompilerParams(
            dimension_semantics=("arbitrary", "arbitrary"), vmem_limit_bytes=VMEM_LIMIT),
        name="df_attention",
    )(slopes, proj, proj, proj, proj, bias, dbias, lq, lk, gsum)


def _out_kernel(x_ref, osb_ref, odf_ref, zsb_ref, zdf_ref, gpre_ref, gpost_ref, ghead_ref,
                wg_ref, bg_ref, wsb_ref, wdf_ref, wo_ref, o_ref, m_ref,
                *, col_chunk, row_block, lambda_init):
    d = x_ref.shape[1]
    for r in range(0, x_ref.shape[0], row_block):
        rows = slice(r, r + row_block)
        x = x_ref[rows, :]
        h = _rms(x, gpre_ref[...]).astype(BF16)
        a_sb = (osb_ref[rows, :] * _silu(zsb_ref[rows, :].astype(F32))).astype(BF16)
        a_df = []
        for hd in range(DF_HEADS):
            cols = slice(hd * LANES, (hd + 1) * LANES)
            o = _rms(odf_ref[rows, cols], ghead_ref[...]) * (1.0 - lambda_init)
            a_df.append((o * _silu(zdf_ref[rows, cols].astype(F32))).astype(BF16))
        a_df = jnp.concatenate(a_df, axis=1)
        for n in range(0, d, col_chunk):
            cols = slice(n, n + col_chunk)
            dcols = slice(d + n, d + n + col_chunk)
            g_sb = jax.nn.sigmoid(_dot(h, wg_ref[:, cols]) + bg_ref[:, cols])
            g_df = jax.nn.sigmoid(_dot(h, wg_ref[:, dcols]) + bg_ref[:, dcols])
            y_sb = _dot(a_sb, wsb_ref[:, cols])
            y_df = _dot(a_df, wdf_ref[:, cols])
            m_ref[rows, cols] = (g_sb * y_sb + g_df * y_df).astype(BF16)
        out = _dot(m_ref[rows, :], wo_ref[...])
        o_ref[rows, :] = x + _rms(out, gpost_ref[...])


def _output_stage(xf, o_sb, o_df, proj, gpre, gpost, ghead, wg, bg, wsb, wdf, wo, lambda_init):
    t, d = xf.shape
    const = lambda i: (0, 0)
    resident = lambda a: pl.BlockSpec(a.shape, const, pipeline_mode=pl.Buffered(1))
    rows = lambda width, col: pl.BlockSpec((TM_OUT, width), lambda i: (i, col))
    return pl.pallas_call(
        functools.partial(_out_kernel, col_chunk=512, row_block=256, lambda_init=lambda_init),
        grid=(t // TM_OUT,),
        in_specs=[
            rows(d, 0),
            rows(SLAB, 0),
            rows(SLAB, 0),
            rows(SLAB, 3),
            rows(SLAB, 7),
            pl.BlockSpec((1, d), const),
            pl.BlockSpec((1, d), const),
            pl.BlockSpec((1, LANES), const),
            resident(wg), pl.BlockSpec(bg.shape, const), resident(wsb), resident(wdf), resident(wo),
        ],
        out_specs=rows(d, 0),
        out_shape=jax.ShapeDtypeStruct((t, d), F32),
        scratch_shapes=[pltpu.VMEM((TM_OUT, d), BF16)],
        compiler_params=pltpu.CompilerParams(
            dimension_semantics=("arbitrary",), vmem_limit_bytes=VMEM_LIMIT),
        name="output_stage",
    )(xf, o_sb, o_df, proj, proj, gpre, gpost, ghead, wg, bg, wsb, wdf, wo)


def _constant_tiles():
    sb = np.arange(SB_TILE)
    tri = (sb[:, None] >= sb[None, :]).astype(np.float32)
    tri2 = np.concatenate([tri, tri], axis=0)
    r = np.arange(TQ)[:, None]
    c = np.arange(TK)[None, :]
    gsum = (np.arange(SLAB)[:, None] // HEAD_DIM == np.arange(LANES)[None, :]).astype(np.float32)
    slopes = np.array([2.0 ** (-8.0 * (h + 1) / DF_HEADS) for h in range(DF_HEADS)],
                      dtype=np.float64) * LOG2E
    rel = (r - c).astype(np.float64)
    bias = -slopes[:, None, None] * rel[None]
    visible = (c // CHUNK) <= (r // CHUNK)
    dbias = np.where(visible[None], -slopes[:, None, None] * np.abs(rel)[None], NEG_BIG)
    block_slopes = slopes * TQ
    return (jnp.asarray(tri2, BF16), jnp.asarray(gsum, BF16), jnp.asarray(block_slopes, F32),
            jnp.asarray(bias, F32), jnp.asarray(dbias, F32))


def kernel(x, norm_pre, norm_post, w_in, w_gate, b_gate, w_branch_sb, w_branch_df, w_out,
           lambda_q, lambda_k, df_head_norm):
    b, s, d = x.shape
    depth = w_in.shape[0]
    assert s % TQ == 0 and TQ == TK and TQ % CHUNK == 0
    assert s % SB_TILE == 0
    assert (b * s) % TM_PROJ == 0 and (b * s) % TM_OUT == 0
    tri2, gsum, block_slopes, bias, dbias = _constant_tiles()
    xf = x.reshape(b * s, d)
    for l in range(depth):
        lambda_init = 0.8 - 0.6 * float(np.exp(-0.3 * l))
        later = [a[l].astype(F32) for a in (w_gate, w_branch_sb, w_branch_df, w_out)]
        proj, (wg, wsb, wdf, wo) = _in_projection(xf, norm_pre[l][None], w_in[l].astype(BF16), later)
        o_sb = _sb_attention(proj, tri2, gsum, b, s)
        o_df = _df_attention(proj, block_slopes, bias, dbias, lambda_q[l].astype(F32),
                             lambda_k[l].astype(F32), gsum, b, s, lambda_init)
        xf = _output_stage(xf, o_sb, o_df, proj, norm_pre[l][None], norm_post[l][None],
                           df_head_norm[l][None].astype(F32), wg, b_gate[l][None], wsb, wdf, wo,
                           lambda_init)
    return xf.reshape(b, s, d)
```

```python
import functools
import math

import numpy as np
import jax
import jax.numpy as jnp
from jax import lax
from jax.experimental import pallas as pl
from jax.experimental.pallas import tpu as pltpu

F32 = jnp.float32
BF16 = jnp.bfloat16

EPS = 1e-6
CHUNK = 64
HEAD_DIM = 64
LANES = 128
N_GROUPS = 4
SLAB = N_GROUPS * LANES
SB_HEADS = SLAB // HEAD_DIM
DF_HEADS = 4
LOG2E = math.log2(math.e)
Q_SCALE = HEAD_DIM ** -0.5 * LOG2E
Q_SLABS = (0, 4)
SB_Q_HALVES = (0, 8)
DF_Q_HALVES = (4, 9)

TQ = 512
TK = 512
SB_TILE = 256
SB_PREV_ROWS = 192
TM_PROJ = 1024
TM_OUT = 1024
NEG_BIG = -1e30
SKIP_LOG2 = 135.0
BOUND_MARGIN = 1.01
MAX_FIXED_REF_LOG2 = 100.0
SOFTPLUS_CLAMP_LOG2 = 100.0

VMEM_LIMIT = 56 * 1024 * 1024


def _nt_dot(a, b):
    return lax.dot_general(a, b, (((1,), (1,)), ((), ())), preferred_element_type=F32)


def _dot(a, b):
    return jnp.dot(a, b, preferred_element_type=F32)


def _rms(x, gain):
    return x * lax.rsqrt(jnp.mean(x * x, axis=-1, keepdims=True) + EPS) * gain


def _silu(z):
    return z * jax.nn.sigmoid(z)


def _lane_halves(x):
    lane = lax.broadcasted_iota(jnp.int32, (1, LANES), 1)
    first = lane < HEAD_DIM
    zero = jnp.zeros_like(x)
    return jnp.where(first, x, zero), jnp.where(first, zero, x)


def _max_sq_group_norms(refs, gsum_ref):
    best = jnp.zeros((1, LANES), F32)
    for r in range(0, refs[0].shape[0], TK):
        xf = sum(ref[r:r + TK, :] for ref in refs).astype(F32)
        n2 = _dot((xf * xf).astype(BF16), gsum_ref[...])
        best = jnp.maximum(best, jnp.max(n2, axis=0, keepdims=True))
    return best


def _logit_bounds(q_half_refs, k_ref, gsum_ref):
    qn = _max_sq_group_norms(q_half_refs, gsum_ref)
    kn = _max_sq_group_norms((k_ref,), gsum_ref)
    return jnp.sqrt(qn * kn) * BOUND_MARGIN


def _lane_pack(column_values):
    lane = lax.broadcasted_iota(jnp.int32, (1, LANES), 1)
    row = jnp.zeros((1, LANES), F32)
    for h, v in enumerate(column_values):
        row = jnp.where(lane == h, v, row)
    return row


def _lane_reduce(row, n, op, fill):
    lane = lax.broadcasted_iota(jnp.int32, (1, LANES), 1)
    return op(jnp.where(lane < n, row, fill))


def _proj_kernel(x_ref, g_ref, w_ref, *rest, row_block):
    n_side = (len(rest) - 1) // 2
    o_ref = rest[n_side]
    for src_ref, dst_ref in zip(rest[:n_side], rest[n_side + 1:]):
        dst_ref[...] = src_ref[...].astype(BF16)
    n_slabs = w_ref.shape[1] // SLAB
    lane = lax.broadcasted_iota(jnp.int32, (1, SLAB), 1)
    first = lax.bitwise_and(lane, LANES - 1) < HEAD_DIM
    for r in range(0, x_ref.shape[0], row_block):
        rows = slice(r, r + row_block)
        h = _rms(x_ref[rows, :], g_ref[...]).astype(BF16)
        for slab in range(n_slabs):
            cols = slice(slab * SLAB, (slab + 1) * SLAB)
            y = _dot(h, w_ref[:, cols])
            if slab in Q_SLABS:
                y = y * Q_SCALE
                extra = n_slabs + Q_SLABS.index(slab)
                o_ref[rows, extra * SLAB:(extra + 1) * SLAB] = jnp.where(first, 0.0, y).astype(BF16)
                y = jnp.where(first, y, 0.0)
            o_ref[rows, cols] = y.astype(BF16)


def _in_projection(xf, gain, w, later_weights):
    t, d = xf.shape
    n = w.shape[1]
    n_out = n + len(Q_SLABS) * SLAB
    steps = t // TM_PROJ
    bands = []
    for a in later_weights:
        assert a.shape[0] % (steps * 16) == 0
        bands.append(pl.BlockSpec((a.shape[0] // steps, a.shape[1]), lambda i: (i, 0)))
    outs = pl.pallas_call(
        functools.partial(_proj_kernel, row_block=512),
        grid=(steps,),
        in_specs=[
            pl.BlockSpec((TM_PROJ, d), lambda i: (i, 0)),
            pl.BlockSpec((1, d), lambda i: (0, 0)),
            pl.BlockSpec((d, n), lambda i: (0, 0), pipeline_mode=pl.Buffered(1)),
        ] + bands,
        out_specs=[pl.BlockSpec((TM_PROJ, n_out), lambda i: (i, 0))] + bands,
        out_shape=[jax.ShapeDtypeStruct((t, n_out), BF16)]
        + [jax.ShapeDtypeStruct(a.shape, BF16) for a in later_weights],
        compiler_params=pltpu.CompilerParams(
            dimension_semantics=("arbitrary",), vmem_limit_bytes=VMEM_LIMIT),
        name="in_projection",
    )(xf, gain, w, *later_weights)
    return outs[0], outs[1:]


def _softplus2(z2):
    return jnp.maximum(z2, jnp.log2(1.0 + jnp.exp2(jnp.minimum(z2, SOFTPLUS_CLAMP_LOG2))))


def _sb_kernel(qa_ref, qb_ref, k_ref, v_ref, tri2_ref, gsum_ref, o_ref, zb_ref):
    i = pl.program_id(1)

    @pl.when(i == 0)
    def _():
        zb_ref[...] = _logit_bounds((qa_ref, qb_ref), k_ref, gsum_ref)

    tile = SB_TILE
    q_half_refs = (qa_ref, qb_ref)

    def block(j, carries, diag, r0=0, r1=SB_TILE):
        start = pl.multiple_of(j * tile, tile)
        q_rows = pl.ds(pl.multiple_of(i * tile, tile) + r0, r1 - r0)
        tri2 = tri2_ref[...]
        if diag:
            row = lax.broadcasted_iota(jnp.int32, (r1 - r0, tile), 0) + r0
            col = lax.broadcasted_iota(jnp.int32, (r1 - r0, tile), 1)
            strict = col < row
        new_carries, pvs = [], []
        for g in range(N_GROUPS):
            cols = slice(g * LANES, (g + 1) * LANES)
            kb = k_ref[pl.ds(start, tile), cols]
            v_heads = _lane_halves(v_ref[pl.ds(start, tile), cols])
            ws = []
            for sub in range(2):
                h = 2 * g + sub
                z = _nt_dot(q_half_refs[sub][q_rows, cols], kb)
                sp = _softplus2(z)
                if diag:
                    sp = jnp.where(strict, sp, 0.0)
                hi = sp.astype(BF16)
                lo = (sp - hi.astype(F32)).astype(BF16)
                suffix = _dot(jnp.concatenate([hi, lo], axis=1), tri2)
                w = jnp.exp2(z - suffix - carries[h])
                if diag:
                    w = jnp.where(strict, w, 0.0)
                new_carries.append(carries[h] + jnp.sum(sp, axis=1, keepdims=True))
                ws.append(w.astype(BF16))
            pvs.append(_dot(jnp.concatenate(ws, axis=1), jnp.concatenate(v_heads, axis=0)))
        return tuple(new_carries), pvs

    def skip_slack(carries):
        least = _lane_pack([jnp.min(c, axis=0, keepdims=True) for c in carries])
        return _lane_reduce(least - zb_ref[...], SB_HEADS, jnp.min, jnp.inf)

    zeros = (jnp.zeros((tile, 1), F32),) * SB_HEADS

    def diagonal_only(_):
        carries, total = block(i, zeros, True)
        for g in range(N_GROUPS):
            o_ref[:, g * LANES:(g + 1) * LANES] = total[g]
        return (jnp.int32(0), skip_slack(carries)) + carries

    def diagonal_and_previous(_):
        p = SB_PREV_ROWS
        carries, total = block(i, zeros, True)
        top, pvs = block(i - 1, tuple(c[:p] for c in carries), False, 0, p)
        for g in range(N_GROUPS):
            cols = slice(g * LANES, (g + 1) * LANES)
            o_ref[:p, cols] = total[g][:p] + pvs[g]
            o_ref[p:, cols] = total[g][p:]
        rest = tuple(c[p:] for c in carries)
        rest_slack = skip_slack(rest)

        def previous_for_rest(_):
            new_rest, pvs = block(i - 1, rest, False, p, tile)
            for g in range(N_GROUPS):
                o_ref[p:, g * LANES:(g + 1) * LANES] += pvs[g]
            return (skip_slack(new_rest),) + new_rest

        fixed = lax.cond(rest_slack < SKIP_LOG2, previous_for_rest,
                         lambda _: (rest_slack,) + rest, 0)
        carries = tuple(jnp.concatenate([a, b], axis=0) for a, b in zip(top, fixed[1:]))
        return (jnp.int32(1), jnp.minimum(skip_slack(top), fixed[0])) + carries

    state = lax.cond(i > 0, diagonal_and_previous, diagonal_only, 0)

    def cond(state):
        return jnp.logical_and(state[0] < i, state[1] < SKIP_LOG2)

    def body(state):
        t = state[0]
        carries, pvs = block(i - 1 - t, state[2:], False)
        for g in range(N_GROUPS):
            o_ref[:, g * LANES:(g + 1) * LANES] += pvs[g]
        return (t + 1, skip_slack(carries)) + carries

    lax.while_loop(cond, body, state)


def _sb_attention(proj, tri2, gsum, batch, seq):
    nq = seq // SB_TILE
    t = proj.shape[0]
    return pl.pallas_call(
        _sb_kernel,
        grid=(batch, nq),
        in_specs=[
            pl.BlockSpec((seq, SLAB), lambda b, i: (b, SB_Q_HALVES[0])),
            pl.BlockSpec((seq, SLAB), lambda b, i: (b, SB_Q_HALVES[1])),
            pl.BlockSpec((seq, SLAB), lambda b, i: (b, 1)),
            pl.BlockSpec((seq, SLAB), lambda b, i: (b, 2)),
            pl.BlockSpec((2 * SB_TILE, SB_TILE), lambda b, i: (0, 0)),
            pl.BlockSpec((SLAB, LANES), lambda b, i: (0, 0)),
        ],
        out_specs=pl.BlockSpec((SB_TILE, SLAB), lambda b, i: (b * nq + i, 0)),
        out_shape=jax.ShapeDtypeStruct((t, SLAB), F32),
        scratch_shapes=[pltpu.VMEM((1, LANES), F32)],
        compiler_params=pltpu.CompilerParams(
            dimension_semantics=("arbitrary", "arbitrary"), vmem_limit_bytes=VMEM_LIMIT),
        name="sb_attention",
    )(proj, proj, proj, proj, tri2, gsum)


def _df_kernel(slope_ref, qa_ref, qb_ref, k_ref, v_ref, bias_ref, dbias_ref, lq_ref, lk_ref,
               gsum_ref, o_ref, acc_ref, plan_ref, lb_ref, *, lambda_init):
    i = pl.program_id(1)
    n_chain = 2 * DF_HEADS

    @pl.when(i == 0)
    def _():
        lbound = _logit_bounds((qa_ref, qb_ref), k_ref, gsum_ref)
        lane = lax.broadcasted_iota(jnp.int32, (1, LANES), 1)
        for c in range(n_chain):
            lb_ref[c] = jnp.max(jnp.where(lane == c, lbound, 0.0))
        worst = 2.0 * _lane_reduce(lbound, n_chain, jnp.max, -jnp.inf)
        plan_ref[0] = (worst <= MAX_FIXED_REF_LOG2).astype(jnp.int32)
        reach = (2.0 * _lane_reduce(lbound, 2, jnp.max, -jnp.inf) + SKIP_LOG2) / slope_ref[0]
        kept = jnp.int32(1)
        for t in range(1, k_ref.shape[0] // TK):
            kept += (reach >= t).astype(jnp.int32)
        plan_ref[1] = kept

    fixed_ref_ok = plan_ref[0] == 1
    n_near = jnp.minimum(plan_ref[1], i)

    q_rows = pl.ds(pl.multiple_of(i * TQ, TQ), TQ)
    q_half_refs = (qa_ref, qb_ref)

    def scores(j, c, bias):
        hd = c // 2
        cols = slice(hd * LANES, (hd + 1) * LANES)
        start = pl.multiple_of(j * TK, TK)
        return _nt_dot(q_half_refs[c % 2][q_rows, cols], k_ref[pl.ds(start, TK), cols]) + bias

    ones = jnp.ones((TK, LANES), BF16)

    def values(j, hd):
        vb = v_ref[pl.ds(pl.multiple_of(j * TK, TK), TK), hd * LANES:(hd + 1) * LANES]
        return jnp.concatenate([vb, ones], axis=1)

    def block_shift(t, hd):
        return slope_ref[hd] * jnp.asarray(t + 1).astype(F32)

    def diagonal(c):
        return jnp.exp2(scores(i, c, dbias_ref[c // 2]) + lb_ref[c]).astype(BF16)

    def earlier(t, c):
        hd = c // 2
        offset = lb_ref[c] - block_shift(t, hd)
        return jnp.exp2(scores(i - 1 - t, c, bias_ref[hd]) + offset).astype(BF16)

    def contract(ps, js, hd):
        return _dot(jnp.concatenate(ps, axis=1), jnp.concatenate([values(j, hd) for j in js], axis=0))

    def start_with(n_merged):
        for c in range(n_chain):
            ps = [diagonal(c)] + [earlier(t, c) for t in range(n_merged)]
            acc_ref[c] = contract(ps, [i - t for t in range(n_merged + 1)], c // 2)

    @pl.when(jnp.logical_and(fixed_ref_ok, i == 0))
    def _():
        start_with(0)

    @pl.when(jnp.logical_and(fixed_ref_ok, i == 1))
    def _():
        start_with(1)

    @pl.when(jnp.logical_and(fixed_ref_ok, i >= 2))
    def _():
        start_with(2)

        def add_blocks(ts, chains):
            for c in chains:
                ps = [earlier(t, c) for t in ts]
                acc_ref[c] += contract(ps, [i - 1 - t for t in ts], c // 2)

        first_far = jnp.maximum(n_near, 2)

        def near(t, carry):
            add_blocks((t,), range(n_chain))
            return carry

        lax.fori_loop(2, first_far, near, 0)

        far_chains = range(2, n_chain)
        n_far = i - first_far

        def far_pair(u, carry):
            add_blocks((first_far + 2 * u, first_far + 2 * u + 1), far_chains)
            return carry

        lax.fori_loop(0, lax.shift_right_logical(n_far, 1), far_pair, 0)

        @pl.when(lax.bitwise_and(n_far, 1) == 1)
        def _():
            add_blocks((i - 1,), far_chains)

    @pl.when(jnp.logical_not(fixed_ref_ok))
    def _():
        ms = []
        for c in range(n_chain):
            s = scores(i, c, dbias_ref[c // 2])
            m = jnp.max(s, axis=1, keepdims=True)
            ms.append(m)
            acc_ref[c] = contract([jnp.exp2(s - m).astype(BF16)], [i], c // 2)

        def body(t, m_run):
            j = i - 1 - t
            m_out = []
            for c in range(n_chain):
                hd = c // 2
                shift = block_shift(t, hd)
                s = scores(j, c, bias_ref[hd])
                m_new = jnp.maximum(m_run[c], jnp.max(s, axis=1, keepdims=True) - shift)
                alpha = jnp.exp2(m_run[c] - m_new)
                p = jnp.exp2(s - (m_new + shift))
                m_out.append(m_new)
                acc_ref[c] = alpha * acc_ref[c] + _dot(p.astype(BF16), values(j, hd))
            return tuple(m_out)

        lax.fori_loop(0, i, body, tuple(ms))

    dots = jnp.sum(lq_ref[...] * lk_ref[...], axis=1, keepdims=True)
    lam = jnp.exp(dots[0:1]) - jnp.exp(dots[1:2]) + lambda_init
    for hd in range(DF_HEADS):
        a0, a1 = acc_ref[2 * hd], acc_ref[2 * hd + 1]
        o_ref[:, hd * LANES:(hd + 1) * LANES] = (
            a0[:, :LANES] / a0[:, LANES:] - lam * (a1[:, :LANES] / a1[:, LANES:]))


def _df_attention(proj, slopes, bias, dbias, lq, lk, gsum, batch, seq, lambda_init):
    nq = seq // TQ
    t = proj.shape[0]
    const3 = lambda b, i: (0, 0, 0)
    const2 = lambda b, i: (0, 0)
    return pl.pallas_call(
        functools.partial(_df_kernel, lambda_init=lambda_init),
        grid=(batch, nq),
        in_specs=[
            pl.BlockSpec(memory_space=pltpu.SMEM),
            pl.BlockSpec((seq, SLAB), lambda b, i: (b, DF_Q_HALVES[0])),
            pl.BlockSpec((seq, SLAB), lambda b, i: (b, DF_Q_HALVES[1])),
            pl.BlockSpec((seq, SLAB), lambda b, i: (b, 5)),
            pl.BlockSpec((seq, SLAB), lambda b, i: (b, 6)),
            pl.BlockSpec((DF_HEADS, TQ, TK), const3),
            pl.BlockSpec((DF_HEADS, TQ, TK), const3),
            pl.BlockSpec((2, HEAD_DIM), const2),
            pl.BlockSpec((2, HEAD_DIM), const2),
            pl.BlockSpec((SLAB, LANES), const2),
        ],
        out_specs=pl.BlockSpec((TQ, SLAB), lambda b, i: (b * nq + i, 0)),
        out_shape=jax.ShapeDtypeStruct((t, SLAB), F32),
        scratch_shapes=[pltpu.VMEM((2 * DF_HEADS, TQ, 2 * LANES), F32),
                        pltpu.SMEM((2,), jnp.int32),
                        pltpu.SMEM((2 * DF_HEADS,), F32)],
        compiler_params=pltpu.CompilerParams(
            dimension_semantics=("arbitrary", "arbitrary"), vmem_limit_bytes=VMEM_LIMIT),
        name="df_attention",
    )(slopes, proj, proj, proj, proj, bias, dbias, lq, lk, gsum)


def _out_kernel(x_ref, osb_ref, odf_ref, zsb_ref, zdf_ref, gpre_ref, gpost_ref, ghead_ref,
                wg_ref, bg_ref, wsb_ref, wdf_ref, wo_ref, o_ref, m_ref,
                *, col_chunk, row_block, lambda_init):
    d = x_ref.shape[1]
    for r in range(0, x_ref.shape[0], row_block):
        rows = slice(r, r + row_block)
        x = x_ref[rows, :]
        h = _rms(x, gpre_ref[...]).astype(BF16)
        a_sb = (osb_ref[rows, :] * _silu(zsb_ref[rows, :].astype(F32))).astype(BF16)
        a_df = []
        for hd in range(DF_HEADS):
            cols = slice(hd * LANES, (hd + 1) * LANES)
            o = _rms(odf_ref[rows, cols], ghead_ref[...]) * (1.0 - lambda_init)
            a_df.append((o * _silu(zdf_ref[rows, cols].astype(F32))).astype(BF16))
        a_df = jnp.concatenate(a_df, axis=1)
        for n in range(0, d, col_chunk):
            cols = slice(n, n + col_chunk)
            dcols = slice(d + n, d + n + col_chunk)
            g_sb = jax.nn.sigmoid(_dot(h, wg_ref[:, cols]) + bg_ref[:, cols])
            g_df = jax.nn.sigmoid(_dot(h, wg_ref[:, dcols]) + bg_ref[:, dcols])
            y_sb = _dot(a_sb, wsb_ref[:, cols])
            y_df = _dot(a_df, wdf_ref[:, cols])
            m_ref[rows, cols] = (g_sb * y_sb + g_df * y_df).astype(BF16)
        out = _dot(m_ref[rows, :], wo_ref[...])
        o_ref[rows, :] = x + _rms(out, gpost_ref[...])


def _output_stage(xf, o_sb, o_df, proj, gpre, gpost, ghead, wg, bg, wsb, wdf, wo, lambda_init):
    t, d = xf.shape
    const = lambda i: (0, 0)
    resident = lambda a: pl.BlockSpec(a.shape, const, pipeline_mode=pl.Buffered(1))
    rows = lambda width, col: pl.BlockSpec((TM_OUT, width), lambda i: (i, col))
    return pl.pallas_call(
        functools.partial(_out_kernel, col_chunk=512, row_block=256, lambda_init=lambda_init),
        grid=(t // TM_OUT,),
        in_specs=[
            rows(d, 0),
            rows(SLAB, 0),
            rows(SLAB, 0),
            rows(SLAB, 3),
            rows(SLAB, 7),
            pl.BlockSpec((1, d), const),
            pl.BlockSpec((1, d), const),
            pl.BlockSpec((1, LANES), const),
            resident(wg), pl.BlockSpec(bg.shape, const), resident(wsb), resident(wdf), resident(wo),
        ],
        out_specs=rows(d, 0),
        out_shape=jax.ShapeDtypeStruct((t, d), F32),
        scratch_shapes=[pltpu.VMEM((TM_OUT, d), BF16)],
        compiler_params=pltpu.CompilerParams(
            dimension_semantics=("arbitrary",), vmem_limit_bytes=VMEM_LIMIT),
        name="output_stage",
    )(xf, o_sb, o_df, proj, proj, gpre, gpost, ghead, wg, bg, wsb, wdf, wo)


def _constant_tiles():
    sb = np.arange(SB_TILE)
    tri = (sb[:, None] >= sb[None, :]).astype(np.float32)
    tri2 = np.concatenate([tri, tri], axis=0)
    r = np.arange(TQ)[:, None]
    c = np.arange(TK)[None, :]
    gsum = (np.arange(SLAB)[:, None] // HEAD_DIM == np.arange(LANES)[None, :]).astype(np.float32)
    slopes = np.array([2.0 ** (-8.0 * (h + 1) / DF_HEADS) for h in range(DF_HEADS)],
                      dtype=np.float64) * LOG2E
    rel = (r - c).astype(np.float64)
    bias = -slopes[:, None, None] * rel[None]
    visible = (c // CHUNK) <= (r // CHUNK)
    dbias = np.where(visible[None], -slopes[:, None, None] * np.abs(rel)[None], NEG_BIG)
    block_slopes = slopes * TQ
    return (jnp.asarray(tri2, BF16), jnp.asarray(gsum, BF16), jnp.asarray(block_slopes, F32),
            jnp.asarray(bias, F32), jnp.asarray(dbias, F32))


def kernel(x, norm_pre, norm_post, w_in, w_gate, b_gate, w_branch_sb, w_branch_df, w_out,
           lambda_q, lambda_k, df_head_norm):
    b, s, d = x.shape
    depth = w_in.shape[0]
    assert s % TQ == 0 and TQ == TK and TQ % CHUNK == 0
    assert s % SB_TILE == 0
    assert (b * s) % TM_PROJ == 0 and (b * s) % TM_OUT == 0
    tri2, gsum, block_slopes, bias, dbias = _constant_tiles()
    xf = x.reshape(b * s, d)
    for l in range(depth):
        lambda_init = 0.8 - 0.6 * float(np.exp(-0.3 * l))
        later = [a[l].astype(F32) for a in (w_gate, w_branch_sb, w_branch_df, w_out)]
        proj, (wg, wsb, wdf, wo) = _in_projection(xf, norm_pre[l][None], w_in[l].astype(BF16), later)
        o_sb = _sb_attention(proj, tri2, gsum, b, s)
        o_df = _df_attention(proj, block_slopes, bias, dbias, lambda_q[l].astype(F32),
                             lambda_k[l].astype(F32), gsum, b, s, lambda_init)
        xf = _output_stage(xf, o_sb, o_df, proj, norm_pre[l][None], norm_post[l][None],
                           df_head_norm[l][None].astype(F32), wg, b_gate[l][None], wsb, wdf, wo,
                           lambda_init)
    return xf.reshape(b, s, d)
```

```python
import functools
import math

import numpy as np
import jax
import jax.numpy as jnp
from jax import lax
from jax.experimental import pallas as pl
from jax.experimental.pallas import tpu as pltpu

F32 = jnp.float32
BF16 = jnp.bfloat16

EPS = 1e-6
CHUNK = 64
HEAD_DIM = 64
LANES = 128
N_GROUPS = 4
SLAB = N_GROUPS * LANES
SB_HEADS = SLAB // HEAD_DIM
DF_HEADS = 4
LOG2E = math.log2(math.e)
Q_SCALE = HEAD_DIM ** -0.5 * LOG2E
Q_SLABS = (0, 4)
SB_Q_HALVES = (0, 8)
DF_Q_HALVES = (4, 9)

TQ = 256
TK = 256
SB_TILE = 256
SB_PREV_ROWS = 192
TM_PROJ = 1024
TM_OUT = 1024
NEG_BIG = -1e30
SKIP_LOG2 = 135.0
BOUND_MARGIN = 1.01
MAX_FIXED_REF_LOG2 = 100.0
SOFTPLUS_CLAMP_LOG2 = 100.0

VMEM_LIMIT = 56 * 1024 * 1024


def _nt_dot(a, b):
    return lax.dot_general(a, b, (((1,), (1,)), ((), ())), preferred_element_type=F32)


def _dot(a, b):
    return jnp.dot(a, b, preferred_element_type=F32)


def _rms(x, gain):
    return x * lax.rsqrt(jnp.mean(x * x, axis=-1, keepdims=True) + EPS) * gain


def _silu(z):
    return z * jax.nn.sigmoid(z)


def _lane_halves(x):
    lane = lax.broadcasted_iota(jnp.int32, (1, LANES), 1)
    first = lane < HEAD_DIM
    zero = jnp.zeros_like(x)
    return jnp.where(first, x, zero), jnp.where(first, zero, x)


def _max_sq_group_norms(refs, gsum_ref):
    best = jnp.zeros((1, LANES), F32)
    for r in range(0, refs[0].shape[0], TK):
        xf = sum(ref[r:r + TK, :] for ref in refs).astype(F32)
        n2 = _dot((xf * xf).astype(BF16), gsum_ref[...])
        best = jnp.maximum(best, jnp.max(n2, axis=0, keepdims=True))
    return best


def _logit_bounds(q_half_refs, k_ref, gsum_ref):
    qn = _max_sq_group_norms(q_half_refs, gsum_ref)
    kn = _max_sq_group_norms((k_ref,), gsum_ref)
    return jnp.sqrt(qn * kn) * BOUND_MARGIN


def _lane_pack(column_values):
    lane = lax.broadcasted_iota(jnp.int32, (1, LANES), 1)
    row = jnp.zeros((1, LANES), F32)
    for h, v in enumerate(column_values):
        row = jnp.where(lane == h, v, row)
    return row


def _lane_reduce(row, n, op, fill):
    lane = lax.broadcasted_iota(jnp.int32, (1, LANES), 1)
    return op(jnp.where(lane < n, row, fill))


def _proj_kernel(x_ref, g_ref, w_ref, *rest, row_block):
    n_side = (len(rest) - 1) // 2
    o_ref = rest[n_side]
    for src_ref, dst_ref in zip(rest[:n_side], rest[n_side + 1:]):
        dst_ref[...] = src_ref[...].astype(BF16)
    n_slabs = w_ref.shape[1] // SLAB
    lane = lax.broadcasted_iota(jnp.int32, (1, SLAB), 1)
    first = lax.bitwise_and(lane, LANES - 1) < HEAD_DIM
    for r in range(0, x_ref.shape[0], row_block):
        rows = slice(r, r + row_block)
        h = _rms(x_ref[rows, :], g_ref[...]).astype(BF16)
        for slab in range(n_slabs):
            cols = slice(slab * SLAB, (slab + 1) * SLAB)
            y = _dot(h, w_ref[:, cols])
            if slab in Q_SLABS:
                y = y * Q_SCALE
                extra = n_slabs + Q_SLABS.index(slab)
                o_ref[rows, extra * SLAB:(extra + 1) * SLAB] = jnp.where(first, 0.0, y).astype(BF16)
                y = jnp.where(first, y, 0.0)
            o_ref[rows, cols] = y.astype(BF16)


def _in_projection(xf, gain, w, later_weights):
    t, d = xf.shape
    n = w.shape[1]
    n_out = n + len(Q_SLABS) * SLAB
    steps = t // TM_PROJ
    bands = []
    for a in later_weights:
        assert a.shape[0] % (steps * 16) == 0
        bands.append(pl.BlockSpec((a.shape[0] // steps, a.shape[1]), lambda i: (i, 0)))
    outs = pl.pallas_call(
        functools.partial(_proj_kernel, row_block=512),
        grid=(steps,),
        in_specs=[
            pl.BlockSpec((TM_PROJ, d), lambda i: (i, 0)),
            pl.BlockSpec((1, d), lambda i: (0, 0)),
            pl.BlockSpec((d, n), lambda i: (0, 0), pipeline_mode=pl.Buffered(1)),
        ] + bands,
        out_specs=[pl.BlockSpec((TM_PROJ, n_out), lambda i: (i, 0))] + bands,
        out_shape=[jax.ShapeDtypeStruct((t, n_out), BF16)]
        + [jax.ShapeDtypeStruct(a.shape, BF16) for a in later_weights],
        compiler_params=pltpu.CompilerParams(
            dimension_semantics=("arbitrary",), vmem_limit_bytes=VMEM_LIMIT),
        name="in_projection",
    )(xf, gain, w, *later_weights)
    return outs[0], outs[1:]


def _softplus2(z2):
    return jnp.maximum(z2, jnp.log2(1.0 + jnp.exp2(jnp.minimum(z2, SOFTPLUS_CLAMP_LOG2))))


def _sb_kernel(qa_ref, qb_ref, k_ref, v_ref, tri2_ref, gsum_ref, o_ref, zb_ref):
    i = pl.program_id(1)

    @pl.when(i == 0)
    def _():
        zb_ref[...] = _logit_bounds((qa_ref, qb_ref), k_ref, gsum_ref)

    tile = SB_TILE
    q_half_refs = (qa_ref, qb_ref)

    def block(j, carries, diag, r0=0, r1=SB_TILE):
        start = pl.multiple_of(j * tile, tile)
        q_rows = pl.ds(pl.multiple_of(i * tile, tile) + r0, r1 - r0)
        tri2 = tri2_ref[...]
        if diag:
            row = lax.broadcasted_iota(jnp.int32, (r1 - r0, tile), 0) + r0
            col = lax.broadcasted_iota(jnp.int32, (r1 - r0, tile), 1)
            strict = col < row
        new_carries, pvs = [], []
        for g in range(N_GROUPS):
            cols = slice(g * LANES, (g + 1) * LANES)
            kb = k_ref[pl.ds(start, tile), cols]
            v_heads = _lane_halves(v_ref[pl.ds(start, tile), cols])
            ws = []
            for sub in range(2):
                h = 2 * g + sub
                z = _nt_dot(q_half_refs[sub][q_rows, cols], kb)
                sp = _softplus2(z)
                if diag:
                    sp = jnp.where(strict, sp, 0.0)
                hi = sp.astype(BF16)
                lo = (sp - hi.astype(F32)).astype(BF16)
                suffix = _dot(jnp.concatenate([hi, lo], axis=1), tri2)
                w = jnp.exp2(z - suffix - carries[h])
                if diag:
                    w = jnp.where(strict, w, 0.0)
                new_carries.append(carries[h] + jnp.sum(sp, axis=1, keepdims=True))
                ws.append(w.astype(BF16))
            pvs.append(_dot(jnp.concatenate(ws, axis=1), jnp.concatenate(v_heads, axis=0)))
        return tuple(new_carries), pvs

    def skip_slack(carries):
        least = _lane_pack([jnp.min(c, axis=0, keepdims=True) for c in carries])
        return _lane_reduce(least - zb_ref[...], SB_HEADS, jnp.min, jnp.inf)

    zeros = (jnp.zeros((tile, 1), F32),) * SB_HEADS

    def diagonal_only(_):
        carries, total = block(i, zeros, True)
        for g in range(N_GROUPS):
            o_ref[:, g * LANES:(g + 1) * LANES] = total[g]
        return (jnp.int32(0), skip_slack(carries)) + carries

    def diagonal_and_previous(_):
        p = SB_PREV_ROWS
        carries, total = block(i, zeros, True)
        top, pvs = block(i - 1, tuple(c[:p] for c in carries), False, 0, p)
        for g in range(N_GROUPS):
            cols = slice(g * LANES, (g + 1) * LANES)
            o_ref[:p, cols] = total[g][:p] + pvs[g]
            o_ref[p:, cols] = total[g][p:]
        rest = tuple(c[p:] for c in carries)
        rest_slack = skip_slack(rest)
        top_slack = skip_slack(top)

        def previous_for_rest(_):
            new_rest, pvs = block(i - 1, rest, False, p, tile)
            for g in range(N_GROUPS):
                o_ref[p:, g * LANES:(g + 1) * LANES] += pvs[g]
            return (skip_slack(new_rest),) + new_rest

        fixed = lax.cond(rest_slack < SKIP_LOG2, previous_for_rest,
                         lambda _: (rest_slack,) + rest, 0)
        carries = tuple(jnp.concatenate([a, b], axis=0) for a, b in zip(top, fixed[1:]))
        return (jnp.int32(1), jnp.minimum(top_slack, fixed[0])) + carries

    state = lax.cond(i > 0, diagonal_and_previous, diagonal_only, 0)

    def cond(state):
        return jnp.logical_and(state[0] < i, state[1] < SKIP_LOG2)

    def body(state):
        t = state[0]
        carries, pvs = block(i - 1 - t, state[2:], False)
        for g in range(N_GROUPS):
            o_ref[:, g * LANES:(g + 1) * LANES] += pvs[g]
        return (t + 1, skip_slack(carries)) + carries

    lax.while_loop(cond, body, state)


def _sb_attention(proj, tri2, gsum, batch, seq):
    nq = seq // SB_TILE
    t = proj.shape[0]
    return pl.pallas_call(
        _sb_kernel,
        grid=(batch, nq),
        in_specs=[
            pl.BlockSpec((seq, SLAB), lambda b, i: (b, SB_Q_HALVES[0])),
            pl.BlockSpec((seq, SLAB), lambda b, i: (b, SB_Q_HALVES[1])),
            pl.BlockSpec((seq, SLAB), lambda b, i: (b, 1)),
            pl.BlockSpec((seq, SLAB), lambda b, i: (b, 2)),
            pl.BlockSpec((2 * SB_TILE, SB_TILE), lambda b, i: (0, 0)),
            pl.BlockSpec((SLAB, LANES), lambda b, i: (0, 0)),
        ],
        out_specs=pl.BlockSpec((SB_TILE, SLAB), lambda b, i: (b * nq + i, 0)),
        out_shape=jax.ShapeDtypeStruct((t, SLAB), F32),
        scratch_shapes=[pltpu.VMEM((1, LANES), F32)],
        compiler_params=pltpu.CompilerParams(
            dimension_semantics=("arbitrary", "arbitrary"), vmem_limit_bytes=VMEM_LIMIT),
        name="sb_attention",
    )(proj, proj, proj, proj, tri2, gsum)


def _df_kernel(slope_ref, qa_ref, qb_ref, k_ref, v_ref, bias_ref, dbias_ref, lq_ref, lk_ref,
               gsum_ref, o_ref, acc_ref, plan_ref, lb_ref, *, lambda_init):
    i = pl.program_id(1)
    n_chain = 2 * DF_HEADS

    @pl.when(i == 0)
    def _():
        lbound = _logit_bounds((qa_ref, qb_ref), k_ref, gsum_ref)
        lane = lax.broadcasted_iota(jnp.int32, (1, LANES), 1)
        for c in range(n_chain):
            lb_ref[c] = jnp.max(jnp.where(lane == c, lbound, 0.0))
        worst = 2.0 * _lane_reduce(lbound, n_chain, jnp.max, -jnp.inf)
        plan_ref[0] = (worst <= MAX_FIXED_REF_LOG2).astype(jnp.int32)
        reach = (2.0 * _lane_reduce(lbound, 2, jnp.max, -jnp.inf) + SKIP_LOG2) / slope_ref[0]
        kept = jnp.int32(1)
        for t in range(1, k_ref.shape[0] // TK):
            kept += (reach >= t).astype(jnp.int32)
        plan_ref[1] = kept

    fixed_ref_ok = plan_ref[0] == 1
    n_near = jnp.minimum(plan_ref[1], i)

    q_rows = pl.ds(pl.multiple_of(i * TQ, TQ), TQ)
    q_half_refs = (qa_ref, qb_ref)

    def scores(j, c, bias):
        hd = c // 2
        cols = slice(hd * LANES, (hd + 1) * LANES)
        start = pl.multiple_of(j * TK, TK)
        return _nt_dot(q_half_refs[c % 2][q_rows, cols], k_ref[pl.ds(start, TK), cols]) + bias

    ones = jnp.ones((TK, LANES), BF16)

    def values(j, hd):
        vb = v_ref[pl.ds(pl.multiple_of(j * TK, TK), TK), hd * LANES:(hd + 1) * LANES]
        return jnp.concatenate([vb, ones], axis=1)

    def block_shift(t, hd):
        return slope_ref[hd] * jnp.asarray(t + 1).astype(F32)

    def diagonal(c):
        return jnp.exp2(scores(i, c, dbias_ref[c // 2]) + lb_ref[c]).astype(BF16)

    def earlier(t, c):
        hd = c // 2
        offset = lb_ref[c] - block_shift(t, hd)
        return jnp.exp2(scores(i - 1 - t, c, bias_ref[hd]) + offset).astype(BF16)

    def contract(ps, js, hd):
        return _dot(jnp.concatenate(ps, axis=1), jnp.concatenate([values(j, hd) for j in js], axis=0))

    def start_with(n_merged):
        for c in range(n_chain):
            ps = [diagonal(c)] + [earlier(t, c) for t in range(n_merged)]
            acc_ref[c] = contract(ps, [i - t for t in range(n_merged + 1)], c // 2)

    @pl.when(jnp.logical_and(fixed_ref_ok, i == 0))
    def _():
        start_with(0)

    @pl.when(jnp.logical_and(fixed_ref_ok, i == 1))
    def _():
        start_with(1)

    @pl.when(jnp.logical_and(fixed_ref_ok, i >= 2))
    def _():
        start_with(2)

        def add_blocks(ts, chains):
            for c in chains:
                ps = [earlier(t, c) for t in ts]
                acc_ref[c] += contract(ps, [i - 1 - t for t in ts], c // 2)

        def head0(t, carry):
            add_blocks((t,), range(2))
            return carry

        lax.fori_loop(2, n_near, head0, 0)

        far_chains = range(2, n_chain)
        first_far = 2
        n_far = i - first_far

        def far_pair(u, carry):
            add_blocks((first_far + 2 * u, first_far + 2 * u + 1), far_chains)
            return carry

        lax.fori_loop(0, lax.shift_right_logical(n_far, 1), far_pair, 0)

        @pl.when(lax.bitwise_and(n_far, 1) == 1)
        def _():
            add_blocks((i - 1,), far_chains)

    @pl.when(jnp.logical_not(fixed_ref_ok))
    def _():
        ms = []
        for c in range(n_chain):
            s = scores(i, c, dbias_ref[c // 2])
            m = jnp.max(s, axis=1, keepdims=True)
            ms.append(m)
            acc_ref[c] = contract([jnp.exp2(s - m).astype(BF16)], [i], c // 2)

        def body(t, m_run):
            j = i - 1 - t
            m_out = []
            for c in range(n_chain):
                hd = c // 2
                shift = block_shift(t, hd)
                s = scores(j, c, bias_ref[hd])
                m_new = jnp.maximum(m_run[c], jnp.max(s, axis=1, keepdims=True) - shift)
                alpha = jnp.exp2(m_run[c] - m_new)
                p = jnp.exp2(s - (m_new + shift))
                m_out.append(m_new)
                acc_ref[c] = alpha * acc_ref[c] + _dot(p.astype(BF16), values(j, hd))
            return tuple(m_out)

        lax.fori_loop(0, i, body, tuple(ms))

    dots = jnp.sum(lq_ref[...] * lk_ref[...], axis=1, keepdims=True)
    lam = jnp.exp(dots[0:1]) - jnp.exp(dots[1:2]) + lambda_init
    for hd in range(DF_HEADS):
        a0, a1 = acc_ref[2 * hd], acc_ref[2 * hd + 1]
        o_ref[:, hd * LANES:(hd + 1) * LANES] = (
            a0[:, :LANES] / a0[:, LANES:] - lam * (a1[:, :LANES] / a1[:, LANES:]))


def _df_attention(proj, slopes, bias, dbias, lq, lk, gsum, batch, seq, lambda_init):
    nq = seq // TQ
    t = proj.shape[0]
    const3 = lambda b, i: (0, 0, 0)
    const2 = lambda b, i: (0, 0)
    return pl.pallas_call(
        functools.partial(_df_kernel, lambda_init=lambda_init),
        grid=(batch, nq),
        in_specs=[
            pl.BlockSpec(memory_space=pltpu.SMEM),
            pl.BlockSpec((seq, SLAB), lambda b, i: (b, DF_Q_HALVES[0])),
            pl.BlockSpec((seq, SLAB), lambda b, i: (b, DF_Q_HALVES[1])),
            pl.BlockSpec((seq, SLAB), lambda b, i: (b, 5)),
            pl.BlockSpec((seq, SLAB), lambda b, i: (b, 6)),
            pl.BlockSpec((DF_HEADS, TQ, TK), const3),
            pl.BlockSpec((DF_HEADS, TQ, TK), const3),
            pl.BlockSpec((2, HEAD_DIM), const2),
            pl.BlockSpec((2, HEAD_DIM), const2),
            pl.BlockSpec((SLAB, LANES), const2),
        ],
        out_specs=pl.BlockSpec((TQ, SLAB), lambda b, i: (b * nq + i, 0)),
        out_shape=jax.ShapeDtypeStruct((t, SLAB), F32),
        scratch_shapes=[pltpu.VMEM((2 * DF_HEADS, TQ, 2 * LANES), F32),
                        pltpu.SMEM((2,), jnp.int32),
                        pltpu.SMEM((2 * DF_HEADS,), F32)],
        compiler_params=pltpu.CompilerParams(
            dimension_semantics=("arbitrary", "arbitrary"), vmem_limit_bytes=VMEM_LIMIT),
        name="df_attention",
    )(slopes, proj, proj, proj, proj, bias, dbias, lq, lk, gsum)


def _out_kernel(x_ref, osb_ref, odf_ref, zsb_ref, zdf_ref, gpre_ref, gpost_ref, ghead_ref,
                wg_ref, bg_ref, wsb_ref, wdf_ref, wo_ref, o_ref, m_ref,
                *, col_chunk, row_block, lambda_init):
    d = x_ref.shape[1]
    for r in range(0, x_ref.shape[0], row_block):
        rows = slice(r, r + row_block)
        x = x_ref[rows, :]
        h = _rms(x, gpre_ref[...]).astype(BF16)
        a_sb = (osb_ref[rows, :] * _silu(zsb_ref[rows, :].astype(F32))).astype(BF16)
        a_df = []
        for hd in range(DF_HEADS):
            cols = slice(hd * LANES, (hd + 1) * LANES)
            o = _rms(odf_ref[rows, cols], ghead_ref[...]) * (1.0 - lambda_init)
            a_df.append((o * _silu(zdf_ref[rows, cols].astype(F32))).astype(BF16))
        a_df = jnp.concatenate(a_df, axis=1)
        for n in range(0, d, col_chunk):
            cols = slice(n, n + col_chunk)
            dcols = slice(d + n, d + n + col_chunk)
            g_sb = jax.nn.sigmoid(_dot(h, wg_ref[:, cols]) + bg_ref[:, cols])
            g_df = jax.nn.sigmoid(_dot(h, wg_ref[:, dcols]) + bg_ref[:, dcols])
            y_sb = _dot(a_sb, wsb_ref[:, cols])
            y_df = _dot(a_df, wdf_ref[:, cols])
            m_ref[rows, cols] = (g_sb * y_sb + g_df * y_df).astype(BF16)
        out = _dot(m_ref[rows, :], wo_ref[...])
        o_ref[rows, :] = x + _rms(out, gpost_ref[...])


def _output_stage(xf, o_sb, o_df, proj, gpre, gpost, ghead, wg, bg, wsb, wdf, wo, lambda_init):
    t, d = xf.shape
    const = lambda i: (0, 0)
    resident = lambda a: pl.BlockSpec(a.shape, const, pipeline_mode=pl.Buffered(1))
    rows = lambda width, col: pl.BlockSpec((TM_OUT, width), lambda i: (i, col))
    return pl.pallas_call(
        functools.partial(_out_kernel, col_chunk=512, row_block=256, lambda_init=lambda_init),
        grid=(t // TM_OUT,),
        in_specs=[
            rows(d, 0),
            rows(SLAB, 0),
            rows(SLAB, 0),
            rows(SLAB, 3),
            rows(SLAB, 7),
            pl.BlockSpec((1, d), const),
            pl.BlockSpec((1, d), const),
            pl.BlockSpec((1, LANES), const),
            resident(wg), pl.BlockSpec(bg.shape, const), resident(wsb), resident(wdf), resident(wo),
        ],
        out_specs=rows(d, 0),
        out_shape=jax.ShapeDtypeStruct((t, d), F32),
        scratch_shapes=[pltpu.VMEM((TM_OUT, d), BF16)],
        compiler_params=pltpu.CompilerParams(
            dimension_semantics=("arbitrary",), vmem_limit_bytes=VMEM_LIMIT),
        name="output_stage",
    )(xf, o_sb, o_df, proj, proj, gpre, gpost, ghead, wg, bg, wsb, wdf, wo)


def _constant_tiles():
    sb = np.arange(SB_TILE)
    tri = (sb[:, None] >= sb[None, :]).astype(np.float32)
    tri2 = np.concatenate([tri, tri], axis=0)
    r = np.arange(TQ)[:, None]
    c = np.arange(TK)[None, :]
    gsum = (np.arange(SLAB)[:, None] // HEAD_DIM == np.arange(LANES)[None, :]).astype(np.float32)
    slopes = np.array([2.0 ** (-8.0 * (h + 1) / DF_HEADS) for h in range(DF_HEADS)],
                      dtype=np.float64) * LOG2E
    rel = (r - c).astype(np.float64)
    bias = -slopes[:, None, None] * rel[None]
    visible = (c // CHUNK) <= (r // CHUNK)
    dbias = np.where(visible[None], -slopes[:, None, None] * np.abs(rel)[None], NEG_BIG)
    block_slopes = slopes * TQ
    return (jnp.asarray(tri2, BF16), jnp.asarray(gsum, BF16), jnp.asarray(block_slopes, F32),
            jnp.asarray(bias, F32), jnp.asarray(dbias, F32))


def kernel(x, norm_pre, norm_post, w_in, w_gate, b_gate, w_branch_sb, w_branch_df, w_out,
           lambda_q, lambda_k, df_head_norm):
    b, s, d = x.shape
    depth = w_in.shape[0]
    assert s % TQ == 0 and TQ == TK and TQ % CHUNK == 0
    assert s % SB_TILE == 0
    assert (b * s) % TM_PROJ == 0 and (b * s) % TM_OUT == 0
    tri2, gsum, block_slopes, bias, dbias = _constant_tiles()
    xf = x.reshape(b * s, d)
    for l in range(depth):
        lambda_init = 0.8 - 0.6 * float(np.exp(-0.3 * l))
        later = [a[l].astype(F32) for a in (w_gate, w_branch_sb, w_branch_df, w_out)]
        proj, (wg, wsb, wdf, wo) = _in_projection(xf, norm_pre[l][None], w_in[l].astype(BF16), later)
        o_sb = _sb_attention(proj, tri2, gsum, b, s)
        o_df = _df_attention(proj, block_slopes, bias, dbias, lambda_q[l].astype(F32),
                             lambda_k[l].astype(F32), gsum, b, s, lambda_init)
        xf = _output_stage(xf, o_sb, o_df, proj, norm_pre[l][None], norm_post[l][None],
                           df_head_norm[l][None].astype(F32), wg, b_gate[l][None], wsb, wdf, wo,
                           lambda_init)
    return xf.reshape(b, s, d)
```

```python
import functools
import math

import numpy as np
import jax
import jax.numpy as jnp
from jax import lax
from jax.experimental import pallas as pl
from jax.experimental.pallas import tpu as pltpu

F32 = jnp.float32
BF16 = jnp.bfloat16

EPS = 1e-6
CHUNK = 64
HEAD_DIM = 64
LANES = 128
N_GROUPS = 4
SLAB = N_GROUPS * LANES
SB_HEADS = SLAB // HEAD_DIM
DF_HEADS = 4
LOG2E = math.log2(math.e)
Q_SCALE = HEAD_DIM ** -0.5 * LOG2E
Q_SLABS = (0, 4)
SB_Q_HALVES = (0, 8)
DF_Q_HALVES = (4, 9)

TQ = 256
TK = 256
SB_TILE = 256
SB_PREV_ROWS = 192
TM_PROJ = 1024
TM_OUT = 1024
NEG_BIG = -1e30
SKIP_LOG2 = 135.0
BOUND_MARGIN = 1.01
MAX_FIXED_REF_LOG2 = 100.0
SOFTPLUS_CLAMP_LOG2 = 100.0

VMEM_LIMIT = 56 * 1024 * 1024


def _nt_dot(a, b):
    return lax.dot_general(a, b, (((1,), (1,)), ((), ())), preferred_element_type=F32)


def _dot(a, b):
    return jnp.dot(a, b, preferred_element_type=F32)


def _rms(x, gain):
    return x * lax.rsqrt(jnp.mean(x * x, axis=-1, keepdims=True) + EPS) * gain


def _silu(z):
    return z * jax.nn.sigmoid(z)


def _lane_halves(x):
    lane = lax.broadcasted_iota(jnp.int32, (1, LANES), 1)
    first = lane < HEAD_DIM
    zero = jnp.zeros_like(x)
    return jnp.where(first, x, zero), jnp.where(first, zero, x)


def _max_sq_group_norms(refs, gsum_ref):
    best = jnp.zeros((1, LANES), F32)
    for r in range(0, refs[0].shape[0], TK):
        xf = sum(ref[r:r + TK, :] for ref in refs).astype(F32)
        n2 = _dot((xf * xf).astype(BF16), gsum_ref[...])
        best = jnp.maximum(best, jnp.max(n2, axis=0, keepdims=True))
    return best


def _logit_bounds(q_half_refs, k_ref, gsum_ref):
    qn = _max_sq_group_norms(q_half_refs, gsum_ref)
    kn = _max_sq_group_norms((k_ref,), gsum_ref)
    return jnp.sqrt(qn * kn) * BOUND_MARGIN


def _lane_pack(column_values):
    lane = lax.broadcasted_iota(jnp.int32, (1, LANES), 1)
    row = jnp.zeros((1, LANES), F32)
    for h, v in enumerate(column_values):
        row = jnp.where(lane == h, v, row)
    return row


def _lane_reduce(row, n, op, fill):
    lane = lax.broadcasted_iota(jnp.int32, (1, LANES), 1)
    return op(jnp.where(lane < n, row, fill))


def _proj_kernel(x_ref, g_ref, w_ref, *rest, row_block):
    n_side = (len(rest) - 1) // 2
    o_ref = rest[n_side]
    for src_ref, dst_ref in zip(rest[:n_side], rest[n_side + 1:]):
        dst_ref[...] = src_ref[...].astype(BF16)
    n_slabs = w_ref.shape[1] // SLAB
    lane = lax.broadcasted_iota(jnp.int32, (1, SLAB), 1)
    first = lax.bitwise_and(lane, LANES - 1) < HEAD_DIM
    for r in range(0, x_ref.shape[0], row_block):
        rows = slice(r, r + row_block)
        h = _rms(x_ref[rows, :], g_ref[...]).astype(BF16)
        for slab in range(n_slabs):
            cols = slice(slab * SLAB, (slab + 1) * SLAB)
            y = _dot(h, w_ref[:, cols])
            if slab in Q_SLABS:
                y = y * Q_SCALE
                extra = n_slabs + Q_SLABS.index(slab)
                o_ref[rows, extra * SLAB:(extra + 1) * SLAB] = jnp.where(first, 0.0, y).astype(BF16)
                y = jnp.where(first, y, 0.0)
            o_ref[rows, cols] = y.astype(BF16)


def _in_projection(xf, gain, w, later_weights):
    t, d = xf.shape
    n = w.shape[1]
    n_out = n + len(Q_SLABS) * SLAB
    steps = t // TM_PROJ
    bands = []
    for a in later_weights:
        assert a.shape[0] % (steps * 16) == 0
        bands.append(pl.BlockSpec((a.shape[0] // steps, a.shape[1]), lambda i: (i, 0)))
    outs = pl.pallas_call(
        functools.partial(_proj_kernel, row_block=512),
        grid=(steps,),
        in_specs=[
            pl.BlockSpec((TM_PROJ, d), lambda i: (i, 0)),
            pl.BlockSpec((1, d), lambda i: (0, 0)),
            pl.BlockSpec((d, n), lambda i: (0, 0), pipeline_mode=pl.Buffered(1)),
        ] + bands,
        out_specs=[pl.BlockSpec((TM_PROJ, n_out), lambda i: (i, 0))] + bands,
        out_shape=[jax.ShapeDtypeStruct((t, n_out), BF16)]
        + [jax.ShapeDtypeStruct(a.shape, BF16) for a in later_weights],
        compiler_params=pltpu.CompilerParams(
            dimension_semantics=("arbitrary",), vmem_limit_bytes=VMEM_LIMIT),
        name="in_projection",
    )(xf, gain, w, *later_weights)
    return outs[0], outs[1:]


def _softplus2(z2):
    return jnp.maximum(z2, jnp.log2(1.0 + jnp.exp2(jnp.minimum(z2, SOFTPLUS_CLAMP_LOG2))))


def _sb_kernel(qa_ref, qb_ref, k_ref, v_ref, tri2_ref, gsum_ref, o_ref, zb_ref):
    i = pl.program_id(1)

    @pl.when(i == 0)
    def _():
        zb_ref[...] = _logit_bounds((qa_ref, qb_ref), k_ref, gsum_ref)

    tile = SB_TILE
    q_half_refs = (qa_ref, qb_ref)

    def block(j, carries, diag, r0=0, r1=SB_TILE):
        start = pl.multiple_of(j * tile, tile)
        q_rows = pl.ds(pl.multiple_of(i * tile, tile) + r0, r1 - r0)
        tri2 = tri2_ref[...]
        if diag:
            row = lax.broadcasted_iota(jnp.int32, (r1 - r0, tile), 0) + r0
            col = lax.broadcasted_iota(jnp.int32, (r1 - r0, tile), 1)
            strict = col < row
        new_carries, pvs = [], []
        for g in range(N_GROUPS):
            cols = slice(g * LANES, (g + 1) * LANES)
            kb = k_ref[pl.ds(start, tile), cols]
            v_heads = _lane_halves(v_ref[pl.ds(start, tile), cols])
            ws = []
            for sub in range(2):
                h = 2 * g + sub
                z = _nt_dot(q_half_refs[sub][q_rows, cols], kb)
                sp = _softplus2(z)
                if diag:
                    sp = jnp.where(strict, sp, 0.0)
                hi = sp.astype(BF16)
                lo = (sp - hi.astype(F32)).astype(BF16)
                suffix = _dot(jnp.concatenate([hi, lo], axis=1), tri2)
                w = jnp.exp2(z - suffix - carries[h])
                if diag:
                    w = jnp.where(strict, w, 0.0)
                new_carries.append(carries[h] + jnp.sum(sp, axis=1, keepdims=True))
                ws.append(w.astype(BF16))
            pvs.append(_dot(jnp.concatenate(ws, axis=1), jnp.concatenate(v_heads, axis=0)))
        return tuple(new_carries), pvs

    def skip_slack(carries):
        least = _lane_pack([jnp.min(c, axis=0, keepdims=True) for c in carries])
        return _lane_reduce(least - zb_ref[...], SB_HEADS, jnp.min, jnp.inf)

    zeros = (jnp.zeros((tile, 1), F32),) * SB_HEADS

    def diagonal_only(_):
        carries, total = block(i, zeros, True)
        for g in range(N_GROUPS):
            o_ref[:, g * LANES:(g + 1) * LANES] = total[g]
        return (jnp.int32(0), skip_slack(carries)) + carries

    def diagonal_and_previous(_):
        p = SB_PREV_ROWS
        carries, total = block(i, zeros, True)
        top, pvs = block(i - 1, tuple(c[:p] for c in carries), False, 0, p)
        for g in range(N_GROUPS):
            cols = slice(g * LANES, (g + 1) * LANES)
            o_ref[:p, cols] = total[g][:p] + pvs[g]
            o_ref[p:, cols] = total[g][p:]
        rest = tuple(c[p:] for c in carries)
        rest_slack = skip_slack(rest)
        top_slack = skip_slack(top)

        def previous_for_rest(_):
            new_rest, pvs = block(i - 1, rest, False, p, tile)
            for g in range(N_GROUPS):
                o_ref[p:, g * LANES:(g + 1) * LANES] += pvs[g]
            return (skip_slack(new_rest),) + new_rest

        fixed = lax.cond(rest_slack < SKIP_LOG2, previous_for_rest,
                         lambda _: (rest_slack,) + rest, 0)
        carries = tuple(jnp.concatenate([a, b], axis=0) for a, b in zip(top, fixed[1:]))
        return (jnp.int32(1), jnp.minimum(top_slack, fixed[0])) + carries

    state = lax.cond(i > 0, diagonal_and_previous, diagonal_only, 0)

    def cond(state):
        return jnp.logical_and(state[0] < i, state[1] < SKIP_LOG2)

    def body(state):
        t = state[0]
        carries, pvs = block(i - 1 - t, state[2:], False)
        for g in range(N_GROUPS):
            o_ref[:, g * LANES:(g + 1) * LANES] += pvs[g]
        return (t + 1, skip_slack(carries)) + carries

    lax.while_loop(cond, body, state)


def _sb_attention(proj, tri2, gsum, batch, seq):
    nq = seq // SB_TILE
    t = proj.shape[0]
    return pl.pallas_call(
        _sb_kernel,
        grid=(batch, nq),
        in_specs=[
            pl.BlockSpec((seq, SLAB), lambda b, i: (b, SB_Q_HALVES[0])),
            pl.BlockSpec((seq, SLAB), lambda b, i: (b, SB_Q_HALVES[1])),
            pl.BlockSpec((seq, SLAB), lambda b, i: (b, 1)),
            pl.BlockSpec((seq, SLAB), lambda b, i: (b, 2)),
            pl.BlockSpec((2 * SB_TILE, SB_TILE), lambda b, i: (0, 0)),
            pl.BlockSpec((SLAB, LANES), lambda b, i: (0, 0)),
        ],
        out_specs=pl.BlockSpec((SB_TILE, SLAB), lambda b, i: (b * nq + i, 0)),
        out_shape=jax.ShapeDtypeStruct((t, SLAB), F32),
        scratch_shapes=[pltpu.VMEM((1, LANES), F32)],
        compiler_params=pltpu.CompilerParams(
            dimension_semantics=("arbitrary", "arbitrary"), vmem_limit_bytes=VMEM_LIMIT),
        name="sb_attention",
    )(proj, proj, proj, proj, tri2, gsum)


def _df_kernel(slope_ref, qa_ref, qb_ref, k_ref, v_ref, bias_ref, dbias_ref, lq_ref, lk_ref,
               gsum_ref, o_ref, acc_ref, plan_ref, lb_ref, *, lambda_init):
    i = pl.program_id(1)
    n_chain = 2 * DF_HEADS

    @pl.when(i == 0)
    def _():
        lbound = _logit_bounds((qa_ref, qb_ref), k_ref, gsum_ref)
        lane = lax.broadcasted_iota(jnp.int32, (1, LANES), 1)
        for c in range(n_chain):
            lb_ref[c] = jnp.max(jnp.where(lane == c, lbound, 0.0))
        worst = 2.0 * _lane_reduce(lbound, n_chain, jnp.max, -jnp.inf)
        plan_ref[0] = (worst <= MAX_FIXED_REF_LOG2).astype(jnp.int32)
        reach = (2.0 * _lane_reduce(lbound, 2, jnp.max, -jnp.inf) + SKIP_LOG2) / slope_ref[0]
        kept = jnp.int32(1)
        for t in range(1, k_ref.shape[0] // TK):
            kept += (reach >= t).astype(jnp.int32)
        plan_ref[1] = kept

    fixed_ref_ok = plan_ref[0] == 1
    n_near = jnp.minimum(plan_ref[1], i)

    q_rows = pl.ds(pl.multiple_of(i * TQ, TQ), TQ)
    q_half_refs = (qa_ref, qb_ref)

    def scores(j, c, bias):
        hd = c // 2
        cols = slice(hd * LANES, (hd + 1) * LANES)
        start = pl.multiple_of(j * TK, TK)
        return _nt_dot(q_half_refs[c % 2][q_rows, cols], k_ref[pl.ds(start, TK), cols]) + bias

    ones = jnp.ones((TK, LANES), BF16)

    def values(j, hd):
        vb = v_ref[pl.ds(pl.multiple_of(j * TK, TK), TK), hd * LANES:(hd + 1) * LANES]
        return jnp.concatenate([vb, ones], axis=1)

    def block_shift(t, hd):
        return slope_ref[hd] * jnp.asarray(t + 1).astype(F32)

    def diagonal(c):
        return jnp.exp2(scores(i, c, dbias_ref[c // 2]) + lb_ref[c]).astype(BF16)

    def earlier(t, c):
        hd = c // 2
        offset = lb_ref[c] - block_shift(t, hd)
        return jnp.exp2(scores(i - 1 - t, c, bias_ref[hd]) + offset).astype(BF16)

    def contract(ps, js, hd):
        return _dot(jnp.concatenate(ps, axis=1), jnp.concatenate([values(j, hd) for j in js], axis=0))

    def start_with(n_merged):
        for c in range(n_chain):
            ps = [diagonal(c)] + [earlier(t, c) for t in range(n_merged)]
            acc_ref[c] = contract(ps, [i - t for t in range(n_merged + 1)], c // 2)

    @pl.when(jnp.logical_and(fixed_ref_ok, i == 0))
    def _():
        start_with(0)

    @pl.when(jnp.logical_and(fixed_ref_ok, i == 1))
    def _():
        start_with(1)

    @pl.when(jnp.logical_and(fixed_ref_ok, i >= 2))
    def _():
        start_with(2)

        def add_blocks(ts, chains):
            for c in chains:
                ps = [earlier(t, c) for t in ts]
                acc_ref[c] += contract(ps, [i - 1 - t for t in ts], c // 2)

        def head0(t, carry):
            add_blocks((t,), range(2))
            return carry

        lax.fori_loop(2, n_near, head0, 0)

        far_chains = range(2, n_chain)
        first_far = 2
        n_far = i - first_far

        def far_four(u, carry):
            t0 = first_far + 4 * u
            add_blocks((t0, t0 + 1, t0 + 2, t0 + 3), far_chains)
            return carry

        n_four = lax.shift_right_logical(n_far, 2)
        lax.fori_loop(0, n_four, far_four, 0)

        @pl.when(lax.bitwise_and(n_far, 2) == 2)
        def _():
            t0 = first_far + 4 * n_four
            add_blocks((t0, t0 + 1), far_chains)

        @pl.when(lax.bitwise_and(n_far, 1) == 1)
        def _():
            add_blocks((i - 1,), far_chains)

    @pl.when(jnp.logical_not(fixed_ref_ok))
    def _():
        ms = []
        for c in range(n_chain):
            s = scores(i, c, dbias_ref[c // 2])
            m = jnp.max(s, axis=1, keepdims=True)
            ms.append(m)
            acc_ref[c] = contract([jnp.exp2(s - m).astype(BF16)], [i], c // 2)

        def body(t, m_run):
            j = i - 1 - t
            m_out = []
            for c in range(n_chain):
                hd = c // 2
                shift = block_shift(t, hd)
                s = scores(j, c, bias_ref[hd])
                m_new = jnp.maximum(m_run[c], jnp.max(s, axis=1, keepdims=True) - shift)
                alpha = jnp.exp2(m_run[c] - m_new)
                p = jnp.exp2(s - (m_new + shift))
                m_out.append(m_new)
                acc_ref[c] = alpha * acc_ref[c] + _dot(p.astype(BF16), values(j, hd))
            return tuple(m_out)

        lax.fori_loop(0, i, body, tuple(ms))

    dots = jnp.sum(lq_ref[...] * lk_ref[...], axis=1, keepdims=True)
    lam = jnp.exp(dots[0:1]) - jnp.exp(dots[1:2]) + lambda_init
    for hd in range(DF_HEADS):
        a0, a1 = acc_ref[2 * hd], acc_ref[2 * hd + 1]
        o_ref[:, hd * LANES:(hd + 1) * LANES] = (
            a0[:, :LANES] / a0[:, LANES:] - lam * (a1[:, :LANES] / a1[:, LANES:]))


def _df_attention(proj, slopes, bias, dbias, lq, lk, gsum, batch, seq, lambda_init):
    nq = seq // TQ
    t = proj.shape[0]
    const3 = lambda b, i: (0, 0, 0)
    const2 = lambda b, i: (0, 0)
    return pl.pallas_call(
        functools.partial(_df_kernel, lambda_init=lambda_init),
        grid=(batch, nq),
        in_specs=[
            pl.BlockSpec(memory_space=pltpu.SMEM),
            pl.BlockSpec((seq, SLAB), lambda b, i: (b, DF_Q_HALVES[0])),
            pl.BlockSpec((seq, SLAB), lambda b, i: (b, DF_Q_HALVES[1])),
            pl.BlockSpec((seq, SLAB), lambda b, i: (b, 5)),
            pl.BlockSpec((seq, SLAB), lambda b, i: (b, 6)),
            pl.BlockSpec((DF_HEADS, TQ, TK), const3),
            pl.BlockSpec((DF_HEADS, TQ, TK), const3),
            pl.BlockSpec((2, HEAD_DIM), const2),
            pl.BlockSpec((2, HEAD_DIM), const2),
            pl.BlockSpec((SLAB, LANES), const2),
        ],
        out_specs=pl.BlockSpec((TQ, SLAB), lambda b, i: (b * nq + i, 0)),
        out_shape=jax.ShapeDtypeStruct((t, SLAB), F32),
        scratch_shapes=[pltpu.VMEM((2 * DF_HEADS, TQ, 2 * LANES), F32),
                        pltpu.SMEM((2,), jnp.int32),
                        pltpu.SMEM((2 * DF_HEADS,), F32)],
        compiler_params=pltpu.CompilerParams(
            dimension_semantics=("arbitrary", "arbitrary"), vmem_limit_bytes=VMEM_LIMIT),
        name="df_attention",
    )(slopes, proj, proj, proj, proj, bias, dbias, lq, lk, gsum)


def _out_kernel(x_ref, osb_ref, odf_ref, zsb_ref, zdf_ref, gpre_ref, gpost_ref, ghead_ref,
                wg_ref, bg_ref, wsb_ref, wdf_ref, wo_ref, o_ref, m_ref,
                *, col_chunk, row_block, lambda_init):
    d = x_ref.shape[1]
    for r in range(0, x_ref.shape[0], row_block):
        rows = slice(r, r + row_block)
        x = x_ref[rows, :]
        h = _rms(x, gpre_ref[...]).astype(BF16)
        a_sb = (osb_ref[rows, :] * _silu(zsb_ref[rows, :].astype(F32))).astype(BF16)
        a_df = []
        for hd in range(DF_HEADS):
            cols = slice(hd * LANES, (hd + 1) * LANES)
            o = _rms(odf_ref[rows, cols], ghead_ref[...]) * (1.0 - lambda_init)
            a_df.append((o * _silu(zdf_ref[rows, cols].astype(F32))).astype(BF16))
        a_df = jnp.concatenate(a_df, axis=1)
        for n in range(0, d, col_chunk):
            cols = slice(n, n + col_chunk)
            dcols = slice(d + n, d + n + col_chunk)
            g_sb = jax.nn.sigmoid(_dot(h, wg_ref[:, cols]) + bg_ref[:, cols])
            g_df = jax.nn.sigmoid(_dot(h, wg_ref[:, dcols]) + bg_ref[:, dcols])
            y_sb = _dot(a_sb, wsb_ref[:, cols])
            y_df = _dot(a_df, wdf_ref[:, cols])
            m_ref[rows, cols] = (g_sb * y_sb + g_df * y_df).astype(BF16)
        out = _dot(m_ref[rows, :], wo_ref[...])
        o_ref[rows, :] = x + _rms(out, gpost_ref[...])


def _output_stage(xf, o_sb, o_df, proj, gpre, gpost, ghead, wg, bg, wsb, wdf, wo, lambda_init):
    t, d = xf.shape
    const = lambda i: (0, 0)
    resident = lambda a: pl.BlockSpec(a.shape, const, pipeline_mode=pl.Buffered(1))
    rows = lambda width, col: pl.BlockSpec((TM_OUT, width), lambda i: (i, col))
    return pl.pallas_call(
        functools.partial(_out_kernel, col_chunk=512, row_block=256, lambda_init=lambda_init),
        grid=(t // TM_OUT,),
        in_specs=[
            rows(d, 0),
            rows(SLAB, 0),
            rows(SLAB, 0),
            rows(SLAB, 3),
            rows(SLAB, 7),
            pl.BlockSpec((1, d), const),
            pl.BlockSpec((1, d), const),
            pl.BlockSpec((1, LANES), const),
            resident(wg), pl.BlockSpec(bg.shape, const), resident(wsb), resident(wdf), resident(wo),
        ],
        out_specs=rows(d, 0),
        out_shape=jax.ShapeDtypeStruct((t, d), F32),
        scratch_shapes=[pltpu.VMEM((TM_OUT, d), BF16)],
        compiler_params=pltpu.CompilerParams(
            dimension_semantics=("arbitrary",), vmem_limit_bytes=VMEM_LIMIT),
        name="output_stage",
    )(xf, o_sb, o_df, proj, proj, gpre, gpost, ghead, wg, bg, wsb, wdf, wo)


def _constant_tiles():
    sb = np.arange(SB_TILE)
    tri = (sb[:, None] >= sb[None, :]).astype(np.float32)
    tri2 = np.concatenate([tri, tri], axis=0)
    r = np.arange(TQ)[:, None]
    c = np.arange(TK)[None, :]
    gsum = (np.arange(SLAB)[:, None] // HEAD_DIM == np.arange(LANES)[None, :]).astype(np.float32)
    slopes = np.array([2.0 ** (-8.0 * (h + 1) / DF_HEADS) for h in range(DF_HEADS)],
                      dtype=np.float64) * LOG2E
    rel = (r - c).astype(np.float64)
    bias = -slopes[:, None, None] * rel[None]
    visible = (c // CHUNK) <= (r // CHUNK)
    dbias = np.where(visible[None], -slopes[:, None, None] * np.abs(rel)[None], NEG_BIG)
    block_slopes = slopes * TQ
    return (jnp.asarray(tri2, BF16), jnp.asarray(gsum, BF16), jnp.asarray(block_slopes, F32),
            jnp.asarray(bias, F32), jnp.asarray(dbias, F32))


def kernel(x, norm_pre, norm_post, w_in, w_gate, b_gate, w_branch_sb, w_branch_df, w_out,
           lambda_q, lambda_k, df_head_norm):
    b, s, d = x.shape
    depth = w_in.shape[0]
    assert s % TQ == 0 and TQ == TK and TQ % CHUNK == 0
    assert s % SB_TILE == 0
    assert (b * s) % TM_PROJ == 0 and (b * s) % TM_OUT == 0
    tri2, gsum, block_slopes, bias, dbias = _constant_tiles()
    xf = x.reshape(b * s, d)
    for l in range(depth):
        lambda_init = 0.8 - 0.6 * float(np.exp(-0.3 * l))
        later = [a[l].astype(F32) for a in (w_gate, w_branch_sb, w_branch_df, w_out)]
        proj, (wg, wsb, wdf, wo) = _in_projection(xf, norm_pre[l][None], w_in[l].astype(BF16), later)
        o_sb = _sb_attention(proj, tri2, gsum, b, s)
        o_df = _df_attention(proj, block_slopes, bias, dbias, lambda_q[l].astype(F32),
                             lambda_k[l].astype(F32), gsum, b, s, lambda_init)
        xf = _output_stage(xf, o_sb, o_df, proj, norm_pre[l][None], norm_post[l][None],
                           df_head_norm[l][None].astype(F32), wg, b_gate[l][None], wsb, wdf, wo,
                           lambda_init)
    return xf.reshape(b, s, d)
```

```python
import functools
import math

import numpy as np
import jax
import jax.numpy as jnp
from jax import lax
from jax.experimental import pallas as pl
from jax.experimental.pallas import tpu as pltpu

F32 = jnp.float32
BF16 = jnp.bfloat16

EPS = 1e-6
CHUNK = 64
HEAD_DIM = 64
LANES = 128
N_GROUPS = 4
SLAB = N_GROUPS * LANES
SB_HEADS = SLAB // HEAD_DIM
DF_HEADS = 4
LOG2E = math.log2(math.e)
Q_SCALE = HEAD_DIM ** -0.5 * LOG2E
Q_SLABS = (0, 4)
SB_Q_HALVES = (0, 8)
DF_Q_HALVES = (4, 9)

TQ = 256
TK = 256
SB_TILE = 256
SB_PREV_ROWS = 192
TM_PROJ = 1024
TM_OUT = 1024
NEG_BIG = -1e30
SKIP_LOG2 = 135.0
BOUND_MARGIN = 1.01
MAX_FIXED_REF_LOG2 = 100.0
SOFTPLUS_CLAMP_LOG2 = 100.0

VMEM_LIMIT = 56 * 1024 * 1024


def _nt_dot(a, b):
    return lax.dot_general(a, b, (((1,), (1,)), ((), ())), preferred_element_type=F32)


def _dot(a, b):
    return jnp.dot(a, b, preferred_element_type=F32)


def _rms(x, gain):
    return x * lax.rsqrt(jnp.mean(x * x, axis=-1, keepdims=True) + EPS) * gain


def _silu(z):
    return z * jax.nn.sigmoid(z)


def _lane_halves(x):
    lane = lax.broadcasted_iota(jnp.int32, (1, LANES), 1)
    first = lane < HEAD_DIM
    zero = jnp.zeros_like(x)
    return jnp.where(first, x, zero), jnp.where(first, zero, x)


def _max_sq_group_norms(refs, gsum_ref):
    best = jnp.zeros((1, LANES), F32)
    for r in range(0, refs[0].shape[0], TK):
        xf = sum(ref[r:r + TK, :] for ref in refs).astype(F32)
        n2 = _dot((xf * xf).astype(BF16), gsum_ref[...])
        best = jnp.maximum(best, jnp.max(n2, axis=0, keepdims=True))
    return best


def _logit_bounds(q_half_refs, k_ref, gsum_ref):
    qn = _max_sq_group_norms(q_half_refs, gsum_ref)
    kn = _max_sq_group_norms((k_ref,), gsum_ref)
    return jnp.sqrt(qn * kn) * BOUND_MARGIN


def _lane_pack(column_values):
    lane = lax.broadcasted_iota(jnp.int32, (1, LANES), 1)
    row = jnp.zeros((1, LANES), F32)
    for h, v in enumerate(column_values):
        row = jnp.where(lane == h, v, row)
    return row


def _lane_reduce(row, n, op, fill):
    lane = lax.broadcasted_iota(jnp.int32, (1, LANES), 1)
    return op(jnp.where(lane < n, row, fill))


def _proj_kernel(x_ref, g_ref, w_ref, *rest, row_block):
    n_side = (len(rest) - 1) // 2
    o_ref = rest[n_side]
    for src_ref, dst_ref in zip(rest[:n_side], rest[n_side + 1:]):
        dst_ref[...] = src_ref[...].astype(BF16)
    n_slabs = w_ref.shape[1] // SLAB
    lane = lax.broadcasted_iota(jnp.int32, (1, SLAB), 1)
    first = lax.bitwise_and(lane, LANES - 1) < HEAD_DIM
    for r in range(0, x_ref.shape[0], row_block):
        rows = slice(r, r + row_block)
        h = _rms(x_ref[rows, :], g_ref[...]).astype(BF16)
        for slab in range(n_slabs):
            cols = slice(slab * SLAB, (slab + 1) * SLAB)
            y = _dot(h, w_ref[:, cols])
            if slab in Q_SLABS:
                y = y * Q_SCALE
                extra = n_slabs + Q_SLABS.index(slab)
                o_ref[rows, extra * SLAB:(extra + 1) * SLAB] = jnp.where(first, 0.0, y).astype(BF16)
                y = jnp.where(first, y, 0.0)
            o_ref[rows, cols] = y.astype(BF16)


def _in_projection(xf, gain, w, later_weights):
    t, d = xf.shape
    n = w.shape[1]
    n_out = n + len(Q_SLABS) * SLAB
    steps = t // TM_PROJ
    bands = []
    for a in later_weights:
        assert a.shape[0] % (steps * 16) == 0
        bands.append(pl.BlockSpec((a.shape[0] // steps, a.shape[1]), lambda i: (i, 0)))
    outs = pl.pallas_call(
        functools.partial(_proj_kernel, row_block=512),
        grid=(steps,),
        in_specs=[
            pl.BlockSpec((TM_PROJ, d), lambda i: (i, 0)),
            pl.BlockSpec((1, d), lambda i: (0, 0)),
            pl.BlockSpec((d, n), lambda i: (0, 0), pipeline_mode=pl.Buffered(1)),
        ] + bands,
        out_specs=[pl.BlockSpec((TM_PROJ, n_out), lambda i: (i, 0))] + bands,
        out_shape=[jax.ShapeDtypeStruct((t, n_out), BF16)]
        + [jax.ShapeDtypeStruct(a.shape, BF16) for a in later_weights],
        compiler_params=pltpu.CompilerParams(
            dimension_semantics=("parallel",), vmem_limit_bytes=VMEM_LIMIT),
        name="in_projection",
    )(xf, gain, w, *later_weights)
    return outs[0], outs[1:]


def _softplus2(z2):
    return jnp.maximum(z2, jnp.log2(1.0 + jnp.exp2(jnp.minimum(z2, SOFTPLUS_CLAMP_LOG2))))


def _sb_kernel(qa_ref, qb_ref, k_ref, v_ref, tri2_ref, gsum_ref, o_ref, zb_ref):
    i = pl.program_id(1)

    @pl.when(i == 0)
    def _():
        zb_ref[...] = _logit_bounds((qa_ref, qb_ref), k_ref, gsum_ref)

    tile = SB_TILE
    q_half_refs = (qa_ref, qb_ref)

    def block(j, carries, diag, r0=0, r1=SB_TILE):
        start = pl.multiple_of(j * tile, tile)
        q_rows = pl.ds(pl.multiple_of(i * tile, tile) + r0, r1 - r0)
        tri2 = tri2_ref[...]
        if diag:
            row = lax.broadcasted_iota(jnp.int32, (r1 - r0, tile), 0) + r0
            col = lax.broadcasted_iota(jnp.int32, (r1 - r0, tile), 1)
            strict = col < row
        new_carries, pvs = [], []
        for g in range(N_GROUPS):
            cols = slice(g * LANES, (g + 1) * LANES)
            kb = k_ref[pl.ds(start, tile), cols]
            v_heads = _lane_halves(v_ref[pl.ds(start, tile), cols])
            ws = []
            for sub in range(2):
                h = 2 * g + sub
                z = _nt_dot(q_half_refs[sub][q_rows, cols], kb)
                sp = _softplus2(z)
                if diag:
                    sp = jnp.where(strict, sp, 0.0)
                hi = sp.astype(BF16)
                lo = (sp - hi.astype(F32)).astype(BF16)
                suffix = _dot(jnp.concatenate([hi, lo], axis=1), tri2)
                w = jnp.exp2(z - suffix - carries[h])
                if diag:
                    w = jnp.where(strict, w, 0.0)
                new_carries.append(carries[h] + jnp.sum(sp, axis=1, keepdims=True))
                ws.append(w.astype(BF16))
            pvs.append(_dot(jnp.concatenate(ws, axis=1), jnp.concatenate(v_heads, axis=0)))
        return tuple(new_carries), pvs

    def skip_slack(carries):
        least = _lane_pack([jnp.min(c, axis=0, keepdims=True) for c in carries])
        return _lane_reduce(least - zb_ref[...], SB_HEADS, jnp.min, jnp.inf)

    zeros = (jnp.zeros((tile, 1), F32),) * SB_HEADS

    def diagonal_only(_):
        carries, total = block(i, zeros, True)
        for g in range(N_GROUPS):
            o_ref[:, g * LANES:(g + 1) * LANES] = total[g]
        return (jnp.int32(0), skip_slack(carries)) + carries

    def diagonal_and_previous(_):
        p = SB_PREV_ROWS
        carries, total = block(i, zeros, True)
        top, pvs = block(i - 1, tuple(c[:p] for c in carries), False, 0, p)
        for g in range(N_GROUPS):
            cols = slice(g * LANES, (g + 1) * LANES)
            o_ref[:p, cols] = total[g][:p] + pvs[g]
            o_ref[p:, cols] = total[g][p:]
        rest = tuple(c[p:] for c in carries)
        rest_slack = skip_slack(rest)
        top_slack = skip_slack(top)

        def previous_for_rest(_):
            new_rest, pvs = block(i - 1, rest, False, p, tile)
            for g in range(N_GROUPS):
                o_ref[p:, g * LANES:(g + 1) * LANES] += pvs[g]
            return (skip_slack(new_rest),) + new_rest

        fixed = lax.cond(rest_slack < SKIP_LOG2, previous_for_rest,
                         lambda _: (rest_slack,) + rest, 0)
        carries = tuple(jnp.concatenate([a, b], axis=0) for a, b in zip(top, fixed[1:]))
        return (jnp.int32(1), jnp.minimum(top_slack, fixed[0])) + carries

    state = lax.cond(i > 0, diagonal_and_previous, diagonal_only, 0)

    def cond(state):
        return jnp.logical_and(state[0] < i, state[1] < SKIP_LOG2)

    def body(state):
        t = state[0]
        carries, pvs = block(i - 1 - t, state[2:], False)
        for g in range(N_GROUPS):
            o_ref[:, g * LANES:(g + 1) * LANES] += pvs[g]
        return (t + 1, skip_slack(carries)) + carries

    lax.while_loop(cond, body, state)


def _sb_attention(proj, tri2, gsum, batch, seq):
    nq = seq // SB_TILE
    t = proj.shape[0]
    return pl.pallas_call(
        _sb_kernel,
        grid=(batch, nq),
        in_specs=[
            pl.BlockSpec((seq, SLAB), lambda b, i: (b, SB_Q_HALVES[0])),
            pl.BlockSpec((seq, SLAB), lambda b, i: (b, SB_Q_HALVES[1])),
            pl.BlockSpec((seq, SLAB), lambda b, i: (b, 1)),
            pl.BlockSpec((seq, SLAB), lambda b, i: (b, 2)),
            pl.BlockSpec((2 * SB_TILE, SB_TILE), lambda b, i: (0, 0)),
            pl.BlockSpec((SLAB, LANES), lambda b, i: (0, 0)),
        ],
        out_specs=pl.BlockSpec((SB_TILE, SLAB), lambda b, i: (b * nq + i, 0)),
        out_shape=jax.ShapeDtypeStruct((t, SLAB), F32),
        scratch_shapes=[pltpu.VMEM((1, LANES), F32)],
        compiler_params=pltpu.CompilerParams(
            dimension_semantics=("parallel", "arbitrary"), vmem_limit_bytes=VMEM_LIMIT),
        name="sb_attention",
    )(proj, proj, proj, proj, tri2, gsum)


def _df_kernel(slope_ref, qa_ref, qb_ref, k_ref, v_ref, bias_ref, dbias_ref, lq_ref, lk_ref,
               gsum_ref, o_ref, acc_ref, plan_ref, lb_ref, *, lambda_init):
    i = pl.program_id(1)
    n_chain = 2 * DF_HEADS

    @pl.when(i == 0)
    def _():
        lbound = _logit_bounds((qa_ref, qb_ref), k_ref, gsum_ref)
        lane = lax.broadcasted_iota(jnp.int32, (1, LANES), 1)
        for c in range(n_chain):
            lb_ref[c] = jnp.max(jnp.where(lane == c, lbound, 0.0))
        worst = 2.0 * _lane_reduce(lbound, n_chain, jnp.max, -jnp.inf)
        plan_ref[0] = (worst <= MAX_FIXED_REF_LOG2).astype(jnp.int32)
        reach = (2.0 * _lane_reduce(lbound, 2, jnp.max, -jnp.inf) + SKIP_LOG2) / slope_ref[0]
        kept = jnp.int32(1)
        for t in range(1, k_ref.shape[0] // TK):
            kept += (reach >= t).astype(jnp.int32)
        plan_ref[1] = kept

    fixed_ref_ok = plan_ref[0] == 1
    n_near = jnp.minimum(plan_ref[1], i)

    q_rows = pl.ds(pl.multiple_of(i * TQ, TQ), TQ)
    q_half_refs = (qa_ref, qb_ref)

    def scores(j, c, bias):
        hd = c // 2
        cols = slice(hd * LANES, (hd + 1) * LANES)
        start = pl.multiple_of(j * TK, TK)
        return _nt_dot(q_half_refs[c % 2][q_rows, cols], k_ref[pl.ds(start, TK), cols]) + bias

    ones = jnp.ones((TK, LANES), BF16)

    def values(j, hd):
        vb = v_ref[pl.ds(pl.multiple_of(j * TK, TK), TK), hd * LANES:(hd + 1) * LANES]
        return jnp.concatenate([vb, ones], axis=1)

    def block_shift(t, hd):
        return slope_ref[hd] * jnp.asarray(t + 1).astype(F32)

    def diagonal(c):
        return jnp.exp2(scores(i, c, dbias_ref[c // 2]) + lb_ref[c]).astype(BF16)

    def earlier(t, c):
        hd = c // 2
        offset = lb_ref[c] - block_shift(t, hd)
        return jnp.exp2(scores(i - 1 - t, c, bias_ref[hd]) + offset).astype(BF16)

    def contract(ps, js, hd):
        return _dot(jnp.concatenate(ps, axis=1), jnp.concatenate([values(j, hd) for j in js], axis=0))

    def start_with(n_merged):
        for c in range(n_chain):
            ps = [diagonal(c)] + [earlier(t, c) for t in range(n_merged)]
            acc_ref[c] = contract(ps, [i - t for t in range(n_merged + 1)], c // 2)

    @pl.when(jnp.logical_and(fixed_ref_ok, i == 0))
    def _():
        start_with(0)

    @pl.when(jnp.logical_and(fixed_ref_ok, i == 1))
    def _():
        start_with(1)

    @pl.when(jnp.logical_and(fixed_ref_ok, i >= 2))
    def _():
        start_with(2)

        def add_blocks(ts, chains):
            for c in chains:
                ps = [earlier(t, c) for t in ts]
                acc_ref[c] += contract(ps, [i - 1 - t for t in ts], c // 2)

        def head0(t, carry):
            add_blocks((t,), range(2))
            return carry

        lax.fori_loop(2, n_near, head0, 0)

        far_chains = range(2, n_chain)
        first_far = 2
        n_far = i - first_far

        def far_four(u, carry):
            t0 = first_far + 4 * u
            add_blocks((t0, t0 + 1, t0 + 2, t0 + 3), far_chains)
            return carry

        n_four = lax.shift_right_logical(n_far, 2)
        lax.fori_loop(0, n_four, far_four, 0)

        @pl.when(lax.bitwise_and(n_far, 2) == 2)
        def _():
            t0 = first_far + 4 * n_four
            add_blocks((t0, t0 + 1), far_chains)

        @pl.when(lax.bitwise_and(n_far, 1) == 1)
        def _():
            add_blocks((i - 1,), far_chains)

    @pl.when(jnp.logical_not(fixed_ref_ok))
    def _():
        ms = []
        for c in range(n_chain):
            s = scores(i, c, dbias_ref[c // 2])
            m = jnp.max(s, axis=1, keepdims=True)
            ms.append(m)
            acc_ref[c] = contract([jnp.exp2(s - m).astype(BF16)], [i], c // 2)

        def body(t, m_run):
            j = i - 1 - t
            m_out = []
            for c in range(n_chain):
                hd = c // 2
                shift = block_shift(t, hd)
                s = scores(j, c, bias_ref[hd])
                m_new = jnp.maximum(m_run[c], jnp.max(s, axis=1, keepdims=True) - shift)
                alpha = jnp.exp2(m_run[c] - m_new)
                p = jnp.exp2(s - (m_new + shift))
                m_out.append(m_new)
                acc_ref[c] = alpha * acc_ref[c] + _dot(p.astype(BF16), values(j, hd))
            return tuple(m_out)

        lax.fori_loop(0, i, body, tuple(ms))

    dots = jnp.sum(lq_ref[...] * lk_ref[...], axis=1, keepdims=True)
    lam = jnp.exp(dots[0:1]) - jnp.exp(dots[1:2]) + lambda_init
    for hd in range(DF_HEADS):
        a0, a1 = acc_ref[2 * hd], acc_ref[2 * hd + 1]
        o_ref[:, hd * LANES:(hd + 1) * LANES] = (
            a0[:, :LANES] / a0[:, LANES:] - lam * (a1[:, :LANES] / a1[:, LANES:]))


def _df_attention(proj, slopes, bias, dbias, lq, lk, gsum, batch, seq, lambda_init):
    nq = seq // TQ
    t = proj.shape[0]
    const3 = lambda b, i: (0, 0, 0)
    const2 = lambda b, i: (0, 0)
    return pl.pallas_call(
        functools.partial(_df_kernel, lambda_init=lambda_init),
        grid=(batch, nq),
        in_specs=[
            pl.BlockSpec(memory_space=pltpu.SMEM),
            pl.BlockSpec((seq, SLAB), lambda b, i: (b, DF_Q_HALVES[0])),
            pl.BlockSpec((seq, SLAB), lambda b, i: (b, DF_Q_HALVES[1])),
            pl.BlockSpec((seq, SLAB), lambda b, i: (b, 5)),
            pl.BlockSpec((seq, SLAB), lambda b, i: (b, 6)),
            pl.BlockSpec((DF_HEADS, TQ, TK), const3),
            pl.BlockSpec((DF_HEADS, TQ, TK), const3),
            pl.BlockSpec((2, HEAD_DIM), const2),
            pl.BlockSpec((2, HEAD_DIM), const2),
            pl.BlockSpec((SLAB, LANES), const2),
        ],
        out_specs=pl.BlockSpec((TQ, SLAB), lambda b, i: (b * nq + i, 0)),
        out_shape=jax.ShapeDtypeStruct((t, SLAB), F32),
        scratch_shapes=[pltpu.VMEM((2 * DF_HEADS, TQ, 2 * LANES), F32),
                        pltpu.SMEM((2,), jnp.int32),
                        pltpu.SMEM((2 * DF_HEADS,), F32)],
        compiler_params=pltpu.CompilerParams(
            dimension_semantics=("parallel", "arbitrary"), vmem_limit_bytes=VMEM_LIMIT),
        name="df_attention",
    )(slopes, proj, proj, proj, proj, bias, dbias, lq, lk, gsum)


def _out_kernel(x_ref, osb_ref, odf_ref, zsb_ref, zdf_ref, gpre_ref, gpost_ref, ghead_ref,
                wg_ref, bg_ref, wsb_ref, wdf_ref, wo_ref, o_ref, m_ref,
                *, col_chunk, row_block, lambda_init):
    d = x_ref.shape[1]
    for r in range(0, x_ref.shape[0], row_block):
        rows = slice(r, r + row_block)
        x = x_ref[rows, :]
        h = _rms(x, gpre_ref[...]).astype(BF16)
        a_sb = (osb_ref[rows, :] * _silu(zsb_ref[rows, :].astype(F32))).astype(BF16)
        a_df = []
        for hd in range(DF_HEADS):
            cols = slice(hd * LANES, (hd + 1) * LANES)
            o = _rms(odf_ref[rows, cols], ghead_ref[...]) * (1.0 - lambda_init)
            a_df.append((o * _silu(zdf_ref[rows, cols].astype(F32))).astype(BF16))
        a_df = jnp.concatenate(a_df, axis=1)
        for n in range(0, d, col_chunk):
            cols = slice(n, n + col_chunk)
            dcols = slice(d + n, d + n + col_chunk)
            g_sb = jax.nn.sigmoid(_dot(h, wg_ref[:, cols]) + bg_ref[:, cols])
            g_df = jax.nn.sigmoid(_dot(h, wg_ref[:, dcols]) + bg_ref[:, dcols])
            y_sb = _dot(a_sb, wsb_ref[:, cols])
            y_df = _dot(a_df, wdf_ref[:, cols])
            m_ref[rows, cols] = (g_sb * y_sb + g_df * y_df).astype(BF16)
        out = _dot(m_ref[rows, :], wo_ref[...])
        o_ref[rows, :] = x + _rms(out, gpost_ref[...])


def _output_stage(xf, o_sb, o_df, proj, gpre, gpost, ghead, wg, bg, wsb, wdf, wo, lambda_init):
    t, d = xf.shape
    const = lambda i: (0, 0)
    resident = lambda a: pl.BlockSpec(a.shape, const, pipeline_mode=pl.Buffered(1))
    rows = lambda width, col: pl.BlockSpec((TM_OUT, width), lambda i: (i, col))
    return pl.pallas_call(
        functools.partial(_out_kernel, col_chunk=512, row_block=256, lambda_init=lambda_init),
        grid=(t // TM_OUT,),
        in_specs=[
            rows(d, 0),
            rows(SLAB, 0),
            rows(SLAB, 0),
            rows(SLAB, 3),
            rows(SLAB, 7),
            pl.BlockSpec((1, d), const),
            pl.BlockSpec((1, d), const),
            pl.BlockSpec((1, LANES), const),
            resident(wg), pl.BlockSpec(bg.shape, const), resident(wsb), resident(wdf), resident(wo),
        ],
        out_specs=rows(d, 0),
        out_shape=jax.ShapeDtypeStruct((t, d), F32),
        scratch_shapes=[pltpu.VMEM((TM_OUT, d), BF16)],
        compiler_params=pltpu.CompilerParams(
            dimension_semantics=("parallel",), vmem_limit_bytes=VMEM_LIMIT),
        name="output_stage",
    )(xf, o_sb, o_df, proj, proj, gpre, gpost, ghead, wg, bg, wsb, wdf, wo)


def _constant_tiles():
    sb = np.arange(SB_TILE)
    tri = (sb[:, None] >= sb[None, :]).astype(np.float32)
    tri2 = np.concatenate([tri, tri], axis=0)
    r = np.arange(TQ)[:, None]
    c = np.arange(TK)[None, :]
    gsum = (np.arange(SLAB)[:, None] // HEAD_DIM == np.arange(LANES)[None, :]).astype(np.float32)
    slopes = np.array([2.0 ** (-8.0 * (h + 1) / DF_HEADS) for h in range(DF_HEADS)],
                      dtype=np.float64) * LOG2E
    rel = (r - c).astype(np.float64)
    bias = -slopes[:, None, None] * rel[None]
    visible = (c // CHUNK) <= (r // CHUNK)
    dbias = np.where(visible[None], -slopes[:, None, None] * np.abs(rel)[None], NEG_BIG)
    block_slopes = slopes * TQ
    return (jnp.asarray(tri2, BF16), jnp.asarray(gsum, BF16), jnp.asarray(block_slopes, F32),
            jnp.asarray(bias, F32), jnp.asarray(dbias, F32))


def kernel(x, norm_pre, norm_post, w_in, w_gate, b_gate, w_branch_sb, w_branch_df, w_out,
           lambda_q, lambda_k, df_head_norm):
    b, s, d = x.shape
    depth = w_in.shape[0]
    assert s % TQ == 0 and TQ == TK and TQ % CHUNK == 0
    assert s % SB_TILE == 0
    assert (b * s) % TM_PROJ == 0 and (b * s) % TM_OUT == 0
    tri2, gsum, block_slopes, bias, dbias = _constant_tiles()
    xf = x.reshape(b * s, d)
    for l in range(depth):
        lambda_init = 0.8 - 0.6 * float(np.exp(-0.3 * l))
        later = [a[l].astype(F32) for a in (w_gate, w_branch_sb, w_branch_df, w_out)]
        proj, (wg, wsb, wdf, wo) = _in_projection(xf, norm_pre[l][None], w_in[l].astype(BF16), later)
        o_sb = _sb_attention(proj, tri2, gsum, b, s)
        o_df = _df_attention(proj, block_slopes, bias, dbias, lambda_q[l].astype(F32),
                             lambda_k[l].astype(F32), gsum, b, s, lambda_init)
        xf = _output_stage(xf, o_sb, o_df, proj, norm_pre[l][None], norm_post[l][None],
                           df_head_norm[l][None].astype(F32), wg, b_gate[l][None], wsb, wdf, wo,
                           lambda_init)
    return xf.reshape(b, s, d)
```
